```python
import jax, jax.numpy as jnp
from jax import lax
import numpy as np

D_MODEL = 4096
BATCH = 1
SEQ = 8192
DEPTH = 2

N_BRANCH = 3
RMS_EPS = 1e-6
LN_EPS = 1e-5
D_CONV = D_MODEL // 4
CONV_WIDTH = 31
ATT_HEADS = 8
ATT_HEAD_DIM = 128
D_ATT = ATT_HEADS * ATT_HEAD_DIM
ATT_SCALE = ATT_HEAD_DIM ** -0.5
IDX_HEADS = 16
IDX_HEAD_DIM = 64
IDX_SCALE = IDX_HEAD_DIM ** -0.5
IDX_W_SCALE = IDX_HEADS ** -0.5
TOPK_MAX = 256
Q_BLOCK = 128
ROPE_THETA = 500000.0
ROPE_FRACTION = 4
D_RWKV = D_MODEL // 2
RWKV_HEAD = 64
RWKV_HEADS = D_RWKV // RWKV_HEAD
LORA_DECAY = 96
LORA_AAA = 96
LORA_MV = 64
LORA_GATE = 256
RWKV_GN_EPS = 64e-5
D_FF = ((8 * D_MODEL + 3 * 256 - 1) // (3 * 256)) * 256

RWKV_SIZES = (D_RWKV, D_RWKV, D_RWKV, LORA_DECAY, LORA_AAA, LORA_GATE)
D_RWKV_IN = 3 * D_RWKV + LORA_DECAY + LORA_AAA + LORA_GATE
IDX_SIZES = (IDX_HEADS * IDX_HEAD_DIM, IDX_HEAD_DIM, IDX_HEADS)
D_IDX_IN = IDX_HEADS * IDX_HEAD_DIM + IDX_HEAD_DIM + IDX_HEADS
IN_SIZES = (2 * D_CONV, 3 * D_ATT, D_IDX_IN, D_RWKV_IN, N_BRANCH * D_MODEL)
D_IN = 2 * D_CONV + 3 * D_ATT + D_IDX_IN + D_RWKV_IN + N_BRANCH * D_MODEL

kernel_name = 'hybrid_conv_dsa_rwkv7_gated_block'


def split_cols(z, sizes):
    offs = np.cumsum(sizes)[:-1].tolist()
    return jnp.split(z, offs, axis=-1)


def rms_norm(x, g):
    xf = x.astype(jnp.float32)
    y = xf * lax.rsqrt(jnp.mean(xf * xf, axis=-1, keepdims=True) + RMS_EPS)
    return (y * g.astype(jnp.float32)).astype(x.dtype)


def layer_norm(x, g, b):
    xf = x.astype(jnp.float32)
    m = jnp.mean(xf, axis=-1, keepdims=True)
    var = jnp.mean(jnp.square(xf - m), axis=-1, keepdims=True)
    return ((xf - m) * lax.rsqrt(var + LN_EPS) * g + b).astype(x.dtype)


def partial_rope(x, pos):
    rd = x.shape[-1] // ROPE_FRACTION
    half = rd // 2
    inv_freq = jnp.power(ROPE_THETA, -jnp.arange(half, dtype=jnp.float32) * (2.0 / rd))
    ang = pos.astype(jnp.float32)[:, None] * inv_freq[None, :]
    cos = jnp.cos(ang)[None, :, None, :]
    sin = jnp.sin(ang)[None, :, None, :]
    x1 = x[..., :half].astype(jnp.float32)
    x2 = x[..., half:rd].astype(jnp.float32)
    rot = jnp.concatenate([x1 * cos - x2 * sin, x2 * cos + x1 * sin], axis=-1).astype(x.dtype)
    return jnp.concatenate([rot, x[..., rd:]], axis=-1)


def token_shift(z, mu):
    prev = jnp.pad(z, ((0, 0), (1, 0), (0, 0)))[:, :-1]
    return z + (prev - z) * mu


def conformer_conv(u, dw_w, dw_b, ln_g, ln_b):
    val, gate = jnp.split(u, 2, axis=-1)
    c = val * jax.nn.sigmoid(gate)
    c = lax.conv_general_dilated(
        c, dw_w[:, None, :], window_strides=(1,), padding=((CONV_WIDTH - 1, 0),),
        dimension_numbers=('NWC', 'WIO', 'NWC'), feature_group_count=D_CONV) + dw_b
    return jax.nn.silu(layer_norm(c, ln_g, ln_b))


def dsa_attention(q, k, v, iq, ik, iw):
    B, L = q.shape[0], q.shape[1]
    top_k = min(TOPK_MAX, L // 4)
    nb = L // Q_BLOCK
    key_pos = jnp.arange(L)
    q_pos = key_pos.reshape(nb, Q_BLOCK)

    def to_blocks(a):
        return jnp.moveaxis(a.reshape((B, nb, Q_BLOCK) + a.shape[2:]), 1, 0)

    gather = jax.vmap(lambda arr, idx: arr[idx])

    def block(args):
        qb, iqb, iwb, pos = args
        logits = jnp.einsum('bqhd,bsd->bqhs', iqb, ik, preferred_element_type=jnp.float32) * IDX_SCALE
        score = jnp.einsum('bqhs,bqh->bqs', jax.nn.relu(logits), iwb.astype(jnp.float32) * IDX_W_SCALE)
        causal = key_pos[None, :] <= pos[:, None]
        score = jnp.where(causal[None], score, -jnp.inf)
        _, idx = lax.top_k(score, top_k)
        valid = idx <= pos[None, :, None]
        ks = gather(k, idx)
        vs = gather(v, idx)
        s = jnp.einsum('bqhd,bqkhd->bhqk', qb, ks, preferred_element_type=jnp.float32) * ATT_SCALE
        s = jnp.where(valid[:, None], s, -jnp.inf)
        p = jax.nn.softmax(s, axis=-1).astype(vs.dtype)
        return jnp.einsum('bhqk,bqkhd->bqhd', p, vs)

    out = lax.map(block, (to_blocks(q), to_blocks(iq), to_blocks(iw), q_pos))
    return jnp.moveaxis(out, 0, 1).reshape(B, L, -1)


def wkv7_scan(r, w, k, v, a, b):
    B, _, H, N = r.shape

    def step(S, inp):
        r_t, w_t, k_t, v_t, a_t, b_t = inp
        sa = jnp.einsum('bhij,bhj->bhi', S, a_t)
        S = S * w_t[:, :, None, :] + sa[..., None] * b_t[:, :, None, :] + v_t[..., None] * k_t[:, :, None, :]
        return S, jnp.einsum('bhij,bhj->bhi', S, r_t)

    xs = tuple(jnp.moveaxis(t, 1, 0) for t in (r, w, k, v, a, b))
    _, y = lax.scan(step, jnp.zeros((B, H, N, N), jnp.float32), xs)
    return jnp.moveaxis(y, 0, 1)


def rwkv7_time_mix(z, mu, w0, w2, a0, a2, g2, k_k, k_a, r_k, ln_g, ln_b, v_first, vres):
    B, L, _ = z.shape
    f32 = jnp.float32
    r, k, v, xw, xa, xg = split_cols(token_shift(z, mu), RWKV_SIZES)
    if vres is not None:
        xv, v_up, v_bias = vres
        v = v + (v_first - v) * jax.nn.sigmoid(v_bias + xv @ v_up)
    w_log = -jax.nn.softplus(-(w0 + jnp.tanh(xw) @ w2)) - 0.5
    decay = jnp.exp(-jnp.exp(w_log.astype(f32)))
    a_lr = jax.nn.sigmoid(a0 + xa @ a2)
    g = jax.nn.sigmoid(xg) @ g2

    def heads(t):
        return t.astype(f32).reshape(B, L, RWKV_HEADS, RWKV_HEAD)

    kk = heads(k * k_k)
    kk = kk / jnp.maximum(jnp.sqrt(jnp.sum(kk * kk, axis=-1, keepdims=True)), 1e-12)
    k = k * (1 + (a_lr - 1) * k_a)
    rh, kh, vh = heads(r), heads(k), heads(v)
    y = wkv7_scan(rh, heads(decay), kh, vh, -kk, kk * heads(a_lr))
    m = jnp.mean(y, axis=-1, keepdims=True)
    var = jnp.mean(jnp.square(y - m), axis=-1, keepdims=True)
    y = ((y - m) * lax.rsqrt(var + RWKV_GN_EPS)).reshape(B, L, D_RWKV) * ln_g + ln_b
    bonus = jnp.sum(rh * kh * r_k, axis=-1, keepdims=True) * vh
    y = (y + bonus.reshape(B, L, D_RWKV)).astype(z.dtype) * g
    return y, v


def setup_inputs(seed: int = 0) -> dict:
    key = jax.random.key(seed)
    keys = iter(jax.random.split(key, 40))

    def nrm(shape, scale):
        return jax.random.normal(next(keys), shape, jnp.float32) * scale

    def uni(shape):
        return jax.random.uniform(next(keys), shape, jnp.float32)

    return {
        'x': nrm((BATCH, SEQ, D_MODEL), 1.0),
        'w_in': nrm((DEPTH, D_MODEL, D_IN), D_MODEL ** -0.5),
        'norm_mix': 1.0 + nrm((DEPTH, D_MODEL), 0.05),
        'dw_weight': nrm((DEPTH, CONV_WIDTH, D_CONV), CONV_WIDTH ** -0.5),
        'dw_bias': nrm((DEPTH, D_CONV), 0.01),
        'conv_ln_g': 1.0 + nrm((DEPTH, D_CONV), 0.05),
        'conv_ln_b': nrm((DEPTH, D_CONV), 0.01),
        'w_conv_out': nrm((DEPTH, D_CONV, D_MODEL), D_CONV ** -0.5),
        'w_att_out': nrm((DEPTH, D_ATT, D_MODEL), D_ATT ** -0.5),
        'rwkv_mu': uni((DEPTH, D_RWKV_IN)),
        'rwkv_w0': -0.5 + nrm((DEPTH, D_RWKV), 0.5),
        'rwkv_w2': nrm((DEPTH, LORA_DECAY, D_RWKV), 0.1 * LORA_DECAY ** -0.5),
        'rwkv_a0': nrm((DEPTH, D_RWKV), 0.1),
        'rwkv_a2': nrm((DEPTH, LORA_AAA, D_RWKV), LORA_AAA ** -0.5),
        'rwkv_g2': nrm((DEPTH, LORA_GATE, D_RWKV), LORA_GATE ** -0.5),
        'rwkv_k_k': 0.85 + nrm((DEPTH, D_RWKV), 0.05),
        'rwkv_k_a': 1.0 + nrm((DEPTH, D_RWKV), 0.05),
        'rwkv_r_k': nrm((DEPTH, RWKV_HEADS, RWKV_HEAD), 0.1),
        'rwkv_ln_g': 1.0 + nrm((DEPTH, D_RWKV), 0.05),
        'rwkv_ln_b': nrm((DEPTH, D_RWKV), 0.01),
        'vres_down': nrm((DEPTH - 1, D_MODEL, LORA_MV), D_MODEL ** -0.5),
        'vres_mu': uni((DEPTH - 1, LORA_MV)),
        'vres_up': nrm((DEPTH - 1, LORA_MV, D_RWKV), LORA_MV ** -0.5),
        'vres_bias': nrm((DEPTH - 1, D_RWKV), 0.1),
        'w_rwkv_out': nrm((DEPTH, D_RWKV, D_MODEL), D_RWKV ** -0.5),
        'w_out': nrm((DEPTH, D_MODEL, D_MODEL), D_MODEL ** -0.5),
        'norm_ffn': 1.0 + nrm((DEPTH, D_MODEL), 0.05),
        'w_ffn_gate': nrm((DEPTH, D_MODEL, D_FF), D_MODEL ** -0.5),
        'w_ffn_up': nrm((DEPTH, D_MODEL, D_FF), D_MODEL ** -0.5),
        'w_ffn_down': nrm((DEPTH, D_FF, D_MODEL), D_FF ** -0.5),
        'norm_final': 1.0 + nrm((D_MODEL,), 0.05),
    }


def reference(x, w_in, norm_mix, dw_weight, dw_bias, conv_ln_g, conv_ln_b, w_conv_out, w_att_out,
              rwkv_mu, rwkv_w0, rwkv_w2, rwkv_a0, rwkv_a2, rwkv_g2, rwkv_k_k, rwkv_k_a, rwkv_r_k,
              rwkv_ln_g, rwkv_ln_b, vres_down, vres_mu, vres_up, vres_bias, w_rwkv_out, w_out,
              norm_ffn, w_ffn_gate, w_ffn_up, w_ffn_down, norm_final):
    B, L, _ = x.shape
    pos = jnp.arange(L)
    v_first = None
    for l in range(DEPTH):
        h = rms_norm(x, norm_mix[l])
        z = h @ w_in[l]
        z_conv, z_att, z_idx, z_rwkv, z_gate = split_cols(z, IN_SIZES)
        a_out = conformer_conv(z_conv, dw_weight[l], dw_bias[l], conv_ln_g[l], conv_ln_b[l]) @ w_conv_out[l]
        q, k, v = [t.reshape(B, L, ATT_HEADS, ATT_HEAD_DIM) for t in jnp.split(z_att, 3, axis=-1)]
        q = partial_rope(q, pos)
        k = partial_rope(k, pos)
        iq, ik, iw = split_cols(z_idx, IDX_SIZES)
        iq = partial_rope(iq.reshape(B, L, IDX_HEADS, IDX_HEAD_DIM), pos)
        ik = partial_rope(ik[:, :, None, :], pos)[:, :, 0, :]
        b_out = dsa_attention(q, k, v, iq, ik, iw) @ w_att_out[l]
        vres = None
        if l > 0:
            xv = token_shift(h @ vres_down[l - 1], vres_mu[l - 1])
            vres = (xv, vres_up[l - 1], vres_bias[l - 1])
        c_mix, v_rwkv = rwkv7_time_mix(z_rwkv, rwkv_mu[l], rwkv_w0[l], rwkv_w2[l], rwkv_a0[l], rwkv_a2[l],
                                       rwkv_g2[l], rwkv_k_k[l], rwkv_k_a[l], rwkv_r_k[l], rwkv_ln_g[l],
                                       rwkv_ln_b[l], v_first, vres)
        if l == 0:
            v_first = v_rwkv
        c_out = c_mix @ w_rwkv_out[l]
        gate = jax.nn.sigmoid(z_gate).reshape(B, L, N_BRANCH, D_MODEL)
        merged = gate[:, :, 0] * a_out + gate[:, :, 1] * b_out + gate[:, :, 2] * c_out
        x = x + merged @ w_out[l]
        h = rms_norm(x, norm_ffn[l])
        x = x + (jax.nn.silu(h @ w_ffn_gate[l]) * (h @ w_ffn_up[l])) @ w_ffn_down[l]
    return rms_norm(x, norm_final)
```

```python
import functools

import jax
import jax.numpy as jnp
import numpy as np
from jax import lax
from jax.experimental import pallas as pl
from jax.experimental.pallas import tpu as pltpu

F32 = jnp.float32
BF16 = jnp.bfloat16

D_MODEL = 4096
RMS_EPS = 1e-6
LN_EPS = 1e-5
D_CONV = D_MODEL // 4
CONV_WIDTH = 31
ATT_HEADS = 8
ATT_HEAD_DIM = 128
D_ATT = ATT_HEADS * ATT_HEAD_DIM
ATT_SCALE = ATT_HEAD_DIM ** -0.5
IDX_HEADS = 16
IDX_HEAD_DIM = 64
IDX_SCALE = IDX_HEAD_DIM ** -0.5
IDX_W_SCALE = IDX_HEADS ** -0.5
TOPK_MAX = 256
ROPE_THETA = 500000.0
ROPE_FRACTION = 4
D_RWKV = D_MODEL // 2
RWKV_HEAD = 64
LORA_DECAY = 96
LORA_AAA = 96
LORA_MV = 64
LORA_GATE = 256
RWKV_GN_EPS = 64e-5
D_FF = ((8 * D_MODEL + 3 * 256 - 1) // (3 * 256)) * 256

LANES = 128
CHUNK = 64
D_FF_PAD = 11264
ZR_XW = 3 * D_RWKV
ZR_XA = ZR_XW + LANES
ZR_XG = ZR_XA + LANES
ZR_XV = ZR_XG + LORA_GATE
ZR_USED = ZR_XV + LANES
ZR_WIDTH = 6912
ZI_IK = IDX_HEADS * IDX_HEAD_DIM
ZI_IW = ZI_IK + LANES
ZI_WIDTH = ZI_IW + LANES
ZB_GATE = 2 * D_CONV + 3 * D_ATT
ZB_WIDTH = ZB_GATE + 3 * D_MODEL
VMEM_LIMIT = 56 * 1024 * 1024
INT_MIN = -2147483648
NEG_BIG = -1e30

NT_DIMS = (((1,), (1,)), ((), ()))
TN_DIMS = (((0,), (0,)), ((), ()))


def _params(sem, vmem=VMEM_LIMIT):
    return pltpu.CompilerParams(dimension_semantics=sem, vmem_limit_bytes=vmem)


def _dot(a, b):
    return jnp.dot(a, b, preferred_element_type=F32)


def _split2(x):
    hi = x.astype(BF16)
    lo = (x - hi.astype(F32)).astype(BF16)
    return hi, lo


def _dot_exact_rhs(x, e):
    h1 = x.astype(BF16)
    r1 = x - h1.astype(F32)
    h2 = r1.astype(BF16)
    h3 = (r1 - h2.astype(F32)).astype(BF16)
    return _dot(h1, e) + _dot(h2, e) + _dot(h3, e)


def _dot_exact_lhs(e, x):
    h1 = x.astype(BF16)
    r1 = x - h1.astype(F32)
    h2 = r1.astype(BF16)
    h3 = (r1 - h2.astype(F32)).astype(BF16)
    return _dot(e, h1) + _dot(e, h2) + _dot(e, h3)


def _dot3(a, b):
    ah, al = _split2(a)
    bh, bl = _split2(b)
    return _dot(ah, bh) + _dot(ah, bl) + _dot(al, bh)


def _sigmoid(x):
    return 1.0 / (1.0 + jnp.exp(-x))


def _rms_kernel(x_ref, g_ref, o_ref):
    x = x_ref[...]
    y = x * lax.rsqrt(jnp.mean(x * x, axis=-1, keepdims=True) + RMS_EPS)
    o_ref[...] = (y * g_ref[...]).astype(o_ref.dtype)


def _rms_norm(x, g, out_dtype):
    L, D = x.shape
    tr = min(256, L)
    return pl.pallas_call(
        _rms_kernel,
        grid=(L // tr,),
        in_specs=[pl.BlockSpec((tr, D), lambda i: (i, 0)),
                  pl.BlockSpec((1, D), lambda i: (0, 0))],
        out_specs=pl.BlockSpec((tr, D), lambda i: (i, 0)),
        out_shape=jax.ShapeDtypeStruct((L, D), out_dtype),
        compiler_params=_params(("parallel",)),
        name="rms_norm",
    )(x, g.reshape(1, D))


def _mm_kernel(a_ref, b_ref, o_ref):
    o_ref[...] = _dot(a_ref[...], b_ref[...]).astype(o_ref.dtype)


def _matmul(a, b, out_dtype, tm, tn, name):
    M, K = a.shape
    N = b.shape[1]
    tm = min(tm, M)
    return pl.pallas_call(
        _mm_kernel,
        grid=(M // tm, N // tn),
        in_specs=[pl.BlockSpec((tm, K), lambda i, j: (i, 0)),
                  pl.BlockSpec((K, tn), lambda i, j: (0, j))],
        out_specs=pl.BlockSpec((tm, tn), lambda i, j: (i, j)),
        out_shape=jax.ShapeDtypeStruct((M, N), out_dtype),
        compiler_params=_params(("parallel", "parallel")),
        name=name,
    )(a, b)


def _mm_res_kernel(a_ref, b_ref, x_ref, o_ref):
    o_ref[...] = x_ref[...] + _dot(a_ref[...], b_ref[...])


def _matmul_residual(a, b, x, tm, tn, name):
    M, K = a.shape
    N = b.shape[1]
    tm = min(tm, M)
    return pl.pallas_call(
        _mm_res_kernel,
        grid=(M // tm, N // tn),
        in_specs=[pl.BlockSpec((tm, K), lambda i, j: (i, 0)),
                  pl.BlockSpec((K, tn), lambda i, j: (0, j)),
                  pl.BlockSpec((tm, tn), lambda i, j: (i, j))],
        out_specs=pl.BlockSpec((tm, tn), lambda i, j: (i, j)),
        out_shape=jax.ShapeDtypeStruct((M, N), F32),
        compiler_params=_params(("parallel", "parallel")),
        name=name,
    )(a, b, x)


def _ffn1_kernel(h_ref, wg_ref, wu_ref, o_ref):
    h = h_ref[...]
    g = _dot(h, wg_ref[...])
    u = _dot(h, wu_ref[...])
    o_ref[...] = (g * _sigmoid(g) * u).astype(o_ref.dtype)


def _ffn1(h, wg, wu, tm, tn):
    M, K = h.shape
    N = wg.shape[1]
    tm = min(tm, M)
    return pl.pallas_call(
        _ffn1_kernel,
        grid=(M // tm, N // tn),
        in_specs=[pl.BlockSpec((tm, K), lambda i, j: (i, 0)),
                  pl.BlockSpec((K, tn), lambda i, j: (0, j)),
                  pl.BlockSpec((K, tn), lambda i, j: (0, j))],
        out_specs=pl.BlockSpec((tm, tn), lambda i, j: (i, j)),
        out_shape=jax.ShapeDtypeStruct((M, N), BF16),
        compiler_params=_params(("parallel", "parallel")),
        name="ffn_gate_up",
    )(h, wg, wu)


def _merge_kernel(a_ref, b_ref, c_ref, wa_ref, wb_ref, wc_ref, g0_ref, g1_ref, g2_ref, o_ref):
    ya = _dot(a_ref[...], wa_ref[...])
    yb = _dot(b_ref[...], wb_ref[...])
    yc = _dot(c_ref[...], wc_ref[...])
    m = (_sigmoid(g0_ref[...].astype(F32)) * ya + _sigmoid(g1_ref[...].astype(F32)) * yb
         + _sigmoid(g2_ref[...].astype(F32)) * yc)
    o_ref[...] = m.astype(o_ref.dtype)


def _merge(a, b, c, wa, wb, wc, zb, tm, tn):
    M = a.shape[0]
    tm = min(tm, M)
    g_off = ZB_GATE // tn
    g_stride = D_MODEL // tn
    return pl.pallas_call(
        _merge_kernel,
        grid=(M // tm, D_MODEL // tn),
        in_specs=[pl.BlockSpec((tm, D_CONV), lambda i, j: (i, 0)),
                  pl.BlockSpec((tm, D_ATT), lambda i, j: (i, 0)),
                  pl.BlockSpec((tm, D_RWKV), lambda i, j: (i, 0)),
                  pl.BlockSpec((D_CONV, tn), lambda i, j: (0, j)),
                  pl.BlockSpec((D_ATT, tn), lambda i, j: (0, j)),
                  pl.BlockSpec((D_RWKV, tn), lambda i, j: (0, j)),
                  pl.BlockSpec((tm, tn), lambda i, j: (i, g_off + j)),
                  pl.BlockSpec((tm, tn), lambda i, j: (i, g_off + g_stride + j)),
                  pl.BlockSpec((tm, tn), lambda i, j: (i, g_off + 2 * g_stride + j))],
        out_specs=pl.BlockSpec((tm, tn), lambda i, j: (i, j)),
        out_shape=jax.ShapeDtypeStruct((M, D_MODEL), BF16),
        compiler_params=_params(("parallel", "parallel")),
        name="gated_merge",
    )(a, b, c, wa, wb, wc, zb, zb, zb)


def _conv_kernel(val_ref, gate_ref, hval_ref, hgate_ref, w_ref, b_ref, lg_ref, lb_ref, o_ref,
                 c_ref, *, T):
    i = pl.program_id(0)
    halo = 32
    hv = hval_ref[...].astype(F32)
    hg = hgate_ref[...].astype(F32)
    hc = hv * _sigmoid(hg)
    c_ref[0:halo, :] = jnp.where(i > 0, hc, 0.0)
    v = val_ref[...].astype(F32)
    g = gate_ref[...].astype(F32)
    c_ref[halo:halo + T, :] = v * _sigmoid(g)
    acc = jnp.zeros((T, D_CONV), F32) + b_ref[...]
    for j in range(CONV_WIDTH):
        off = halo - (CONV_WIDTH - 1) + j
        acc = acc + c_ref[off:off + T, :] * w_ref[j:j + 1, :]
    m = jnp.mean(acc, axis=-1, keepdims=True)
    d = acc - m
    var = jnp.mean(d * d, axis=-1, keepdims=True)
    y = d * lax.rsqrt(var + LN_EPS) * lg_ref[...] + lb_ref[...]
    o_ref[...] = (y * _sigmoid(y)).astype(o_ref.dtype)


def _conformer_conv(zb, dw_w, dw_b, ln_g, ln_b):
    L = zb.shape[0]
    T = min(256, L)
    hb = T // 32
    w_pad = jnp.zeros((32, D_CONV), F32).at[:CONV_WIDTH].set(dw_w)
    return pl.pallas_call(
        functools.partial(_conv_kernel, T=T),
        grid=(L // T,),
        in_specs=[pl.BlockSpec((T, D_CONV), lambda i: (i, 0)),
                  pl.BlockSpec((T, D_CONV), lambda i: (i, 1)),
                  pl.BlockSpec((32, D_CONV), lambda i: (jnp.maximum(i * hb - 1, 0), 0)),
                  pl.BlockSpec((32, D_CONV), lambda i: (jnp.maximum(i * hb - 1, 0), 1)),
                  pl.BlockSpec((32, D_CONV), lambda i: (0, 0)),
                  pl.BlockSpec((1, D_CONV), lambda i: (0, 0)),
                  pl.BlockSpec((1, D_CONV), lambda i: (0, 0)),
                  pl.BlockSpec((1, D_CONV), lambda i: (0, 0))],
        out_specs=pl.BlockSpec((T, D_CONV), lambda i: (i, 0)),
        out_shape=jax.ShapeDtypeStruct((L, D_CONV), BF16),
        scratch_shapes=[pltpu.VMEM((T + 32, D_CONV), F32)],
        compiler_params=_params(("parallel",)),
        name="conformer_conv",
    )(zb, zb, zb, zb, w_pad, dw_b.reshape(1, -1), ln_g.reshape(1, -1), ln_b.reshape(1, -1))


def _rope_tables(pos, inv_freq, head_dim):
    rd = head_dim // ROPE_FRACTION
    half = rd // 2
    ang = pos * inv_freq
    cos = jnp.cos(ang)
    sin = jnp.sin(ang)
    lane = lax.broadcasted_iota(jnp.int32, ang.shape, 1) % head_dim
    c = jnp.where(lane < rd, cos, 1.0)
    s_up = jnp.where(lane < half, -sin, 0.0)
    s_dn = jnp.where((lane >= half) & (lane < rd), sin, 0.0)
    return c, s_up, s_dn


def _rope_apply(x, tabs, half):
    c, s_up, s_dn = tabs
    return (x * c + pltpu.roll(x, LANES - half, 1) * s_up + pltpu.roll(x, half, 1) * s_dn)


def _attprep_kernel(q_ref, k_ref, zi_ref, fq_ref, fi_ref, qo_ref, ko_ref, iqo_ref, iko_ref,
                    iwo_ref, *, T):
    i = pl.program_id(0)
    pos = (i * T + lax.broadcasted_iota(jnp.int32, (T, LANES), 0)).astype(F32)
    tq = _rope_tables(pos, fq_ref[...], ATT_HEAD_DIM)
    ti = _rope_tables(pos, fi_ref[...], IDX_HEAD_DIM)
    hq = ATT_HEAD_DIM // ROPE_FRACTION // 2
    hi = IDX_HEAD_DIM // ROPE_FRACTION // 2
    for h in range(ATT_HEADS):
        sl = slice(h * LANES, (h + 1) * LANES)
        qo_ref[:, sl] = (_rope_apply(q_ref[:, sl].astype(F32), tq, hq) * ATT_SCALE).astype(BF16)
        ko_ref[:, sl] = _rope_apply(k_ref[:, sl].astype(F32), tq, hq).astype(BF16)
    for p in range(ZI_IK // LANES):
        sl = slice(p * LANES, (p + 1) * LANES)
        iqo_ref[:, sl] = _rope_apply(zi_ref[:, sl], ti, hi).astype(BF16)
    iko_ref[...] = _rope_apply(zi_ref[:, ZI_IK:ZI_IW], ti, hi).astype(BF16)
    iwo_ref[...] = zi_ref[:, ZI_IW:ZI_WIDTH] * (IDX_W_SCALE * IDX_SCALE)


def _lane_inv_freq(head_dim):
    rd = head_dim // ROPE_FRACTION
    half = rd // 2
    inv_freq = jnp.power(ROPE_THETA, -jnp.arange(half, dtype=F32) * (2.0 / rd))
    lane = np.arange(LANES) % head_dim
    idx = np.where(lane < rd, lane % half, 0)
    return jnp.where(jnp.asarray(lane < rd), inv_freq[idx], 0.0).reshape(1, LANES)


def _att_prep(zb, zi):
    L = zb.shape[0]
    T = min(256, L)
    row = lambda i: (i, 0)
    return pl.pallas_call(
        functools.partial(_attprep_kernel, T=T),
        grid=(L // T,),
        in_specs=[pl.BlockSpec((T, D_ATT), lambda i: (i, 2)),
                  pl.BlockSpec((T, D_ATT), lambda i: (i, 3)),
                  pl.BlockSpec((T, ZI_WIDTH), row),
                  pl.BlockSpec((1, LANES), lambda i: (0, 0)),
                  pl.BlockSpec((1, LANES), lambda i: (0, 0))],
        out_specs=[pl.BlockSpec((T, D_ATT), row), pl.BlockSpec((T, D_ATT), row),
                   pl.BlockSpec((T, ZI_IK), row), pl.BlockSpec((T, LANES), row),
                   pl.BlockSpec((T, LANES), row)],
        out_shape=[jax.ShapeDtypeStruct((L, D_ATT), BF16), jax.ShapeDtypeStruct((L, D_ATT), BF16),
                   jax.ShapeDtypeStruct((L, ZI_IK), BF16), jax.ShapeDtypeStruct((L, LANES), BF16),
                   jax.ShapeDtypeStruct((L, LANES), F32)],
        compiler_params=_params(("parallel",)),
        name="att_prep",
    )(zb, zb, zi, _lane_inv_freq(ATT_HEAD_DIM), _lane_inv_freq(IDX_HEAD_DIM))


def _dsa_kernel(q_ref, iq_ref, iw_ref, ik_ref, k_ref, v_ref, o_ref,
                key_ref, wb_ref, iqm_ref, thr_ref, m_ref, l_ref, acc_ref, *, Q, S, RB, topk):
    i = pl.program_id(0)
    j = pl.program_id(1)
    q_end = (i + 1) * Q

    @pl.when(j == 0)
    def _select():
        lane = lax.broadcasted_iota(jnp.int32, (Q, LANES), 1)
        lo = lane < IDX_HEAD_DIM
        zero = jnp.zeros((Q, LANES), BF16)
        for p in range(IDX_HEADS // 2):
            x = iq_ref[:, p * LANES:(p + 1) * LANES]
            iqm_ref[2 * p] = jnp.where(lo, x, zero)
            iqm_ref[2 * p + 1] = jnp.where(lo, zero, x)
        w = iw_ref[...]
        for h in range(IDX_HEADS):
            wb_ref[h] = jnp.broadcast_to(w[:, h:h + 1], (Q, LANES))
        row = i * Q + lax.broadcasted_iota(jnp.int32, (Q, S), 0)

        def score_chunk(c, carry):
            off = pl.multiple_of(c * S, S)
            ikc = ik_ref[pl.ds(off, S), :]
            sc = jnp.zeros((Q, S), F32)
            for h in range(IDX_HEADS):
                lg = lax.dot_general(iqm_ref[h], ikc, NT_DIMS, preferred_element_type=F32)
                sc = sc + jnp.maximum(lg, 0.0) * jnp.tile(wb_ref[h], (1, S // LANES))
            sc = jnp.where(sc == 0.0, 0.0, sc)
            bits = pltpu.bitcast(sc, jnp.int32)
            key = bits ^ ((bits >> 31) & 0x7FFFFFFF)
            col = off + lax.broadcasted_iota(jnp.int32, (Q, S), 1)
            key_ref[:, pl.ds(off, S)] = jnp.where(col <= row, key, INT_MIN)
            return carry

        lax.fori_loop(0, (q_end + S - 1) // S, score_chunk, 0)

        ncol = q_end // LANES

        def row_block(rb, carry):
            rows = pl.ds(pl.multiple_of(rb * RB, RB), RB)

            def bit_step(b, prefix):
                cand_u = prefix | jnp.left_shift(jnp.int32(1), 31 - b)
                cand = jnp.broadcast_to(cand_u ^ INT_MIN, (RB, LANES))

                def count(c, cnt):
                    kc = key_ref[rows, pl.ds(pl.multiple_of(c * LANES, LANES), LANES)]
                    return cnt + jnp.where(kc >= cand, 1.0, 0.0)

                cnt = lax.fori_loop(0, ncol, count, jnp.zeros((RB, LANES), F32))
                tot = jnp.sum(cnt, axis=1, keepdims=True)
                return jnp.where(tot >= float(topk), cand_u, prefix)

            prefix = lax.fori_loop(0, 32, bit_step, jnp.zeros((RB, 1), jnp.int32))
            thr = jnp.maximum(prefix ^ INT_MIN, INT_MIN + 1)
            thr_ref[rows, :] = jnp.broadcast_to(thr, (RB, LANES))
            return carry

        lax.fori_loop(0, Q // RB, row_block, 0)
        m_ref[...] = jnp.full(m_ref.shape, NEG_BIG, F32)
        l_ref[...] = jnp.zeros(l_ref.shape, F32)
        acc_ref[...] = jnp.zeros(acc_ref.shape, F32)

    @pl.when(j * S < q_end)
    def _attend():
        keys = key_ref[:, pl.ds(pl.multiple_of(j * S, S), S)]
        mask = keys >= jnp.tile(thr_ref[...], (1, S // LANES))
        for h in range(ATT_HEADS):
            sl = slice(h * LANES, (h + 1) * LANES)
            s = lax.dot_general(q_ref[:, sl], k_ref[:, sl], NT_DIMS, preferred_element_type=F32)
            s = jnp.where(mask, s, NEG_BIG)
            m_prev = m_ref[h]
            m_new = jnp.maximum(m_prev, jnp.max(s, axis=1, keepdims=True))
            alpha = jnp.exp(m_prev - m_new)
            p = jnp.where(mask, jnp.exp(s - jnp.tile(m_new, (1, S // LANES))), 0.0)
            l_ref[h] = alpha * l_ref[h] + jnp.sum(p, axis=1, keepdims=True)
            acc_ref[:, sl] = alpha * acc_ref[:, sl] + _dot(p.astype(BF16), v_ref[:, sl])
            m_ref[h] = m_new

    @pl.when(j == pl.num_programs(1) - 1)
    def _finish():
        for h in range(ATT_HEADS):
            sl = slice(h * LANES, (h + 1) * LANES)
            o_ref[:, sl] = (acc_ref[:, sl] / l_ref[h]).astype(o_ref.dtype)


def _dsa_attention(qr, iqr, iws, ik2, kr, zb):
    L = qr.shape[0]
    Q = min(256, L)
    S = min(512, L)
    RB = 64
    topk = min(TOPK_MAX, L // 4)
    nk = L // S

    def kv_block(i, j):
        return jnp.minimum(j, ((i + 1) * Q - 1) // S)

    return pl.pallas_call(
        functools.partial(_dsa_kernel, Q=Q, S=S, RB=RB, topk=topk),
        grid=(L // Q, nk),
        in_specs=[pl.BlockSpec((Q, D_ATT), lambda i, j: (i, 0)),
                  pl.BlockSpec((Q, ZI_IK), lambda i, j: (i, 0)),
                  pl.BlockSpec((Q, LANES), lambda i, j: (i, 0)),
                  pl.BlockSpec((L, LANES), lambda i, j: (0, 0)),
                  pl.BlockSpec((S, D_ATT), lambda i, j: (kv_block(i, j), 0)),
                  pl.BlockSpec((S, D_ATT), lambda i, j: (kv_block(i, j), 4))],
        out_specs=pl.BlockSpec((Q, D_ATT), lambda i, j: (i, 0)),
        out_shape=jax.ShapeDtypeStruct((L, D_ATT), BF16),
        scratch_shapes=[pltpu.VMEM((Q, L), jnp.int32),
                        pltpu.VMEM((IDX_HEADS, Q, LANES), F32),
                        pltpu.VMEM((IDX_HEADS, Q, LANES), BF16),
                        pltpu.VMEM((Q, LANES), jnp.int32),
                        pltpu.VMEM((ATT_HEADS, Q, LANES), F32),
                        pltpu.VMEM((ATT_HEADS, Q, LANES), F32),
                        pltpu.VMEM((Q, D_ATT), F32)],
        compiler_params=_params(("parallel", "arbitrary")),
        name="dsa_attention",
    )(qr, iqr, iws, ik2, kr, zb)


def _head_sum(x, e):
    cols = []
    for p in range(x.shape[1] // LANES):
        cols.append(_dot_exact_rhs(x[:, p * LANES:(p + 1) * LANES], e))
    return jnp.concatenate(cols, axis=1)


def _rwkv_prep_kernel(*refs, T, has_vres):
    if has_vres:
        (z_ref, halo_ref, mu_ref, w0_ref, w2_ref, a0_ref, a2_ref, g2_ref, kk_ref, ka_ref, rk_ref,
         vf_ref, vup_ref, vb_ref,
         rt_ref, at_ref, bt_ref, kt_ref, v_ref, bonus_ref, g_ref, pc_ref) = refs
    else:
        (z_ref, halo_ref, mu_ref, w0_ref, w2_ref, a0_ref, a2_ref, g2_ref, kk_ref, ka_ref, rk_ref,
         rt_ref, at_ref, bt_ref, kt_ref, v_ref, bonus_ref, g_ref, pc_ref) = refs
    i = pl.program_id(0)
    D = D_RWKV

    def shifted(lo, hi):
        z = z_ref[:, lo:hi]
        first = jnp.where(i > 0, halo_ref[7:8, lo:hi], 0.0)
        rows = lax.broadcasted_iota(jnp.int32, z.shape, 0)
        prev = jnp.where(rows == 0, first, pltpu.roll(z, 1, 0))
        return z + (prev - z) * mu_ref[:, lo:hi]

    r = shifted(0, D)
    kraw = shifted(D, 2 * D)
    v = shifted(2 * D, 3 * D)
    xw = shifted(ZR_XW, ZR_XA)
    xa = shifted(ZR_XA, ZR_XG)
    xg = shifted(ZR_XG, ZR_XV)
    if has_vres:
        xv = shifted(ZR_XV, ZR_USED)
        mix = _sigmoid(vb_ref[...] + _dot(xv.astype(BF16), vup_ref[...]))
        v = v + (vf_ref[...] - v) * mix
    v_ref[...] = v

    w_in = w0_ref[...] + _dot(jnp.tanh(xw).astype(BF16), w2_ref[...])
    w_log = -(jnp.maximum(-w_in, 0.0) + jnp.log(1.0 + jnp.exp(-jnp.abs(w_in)))) - 0.5
    logw = -jnp.exp(w_log)
    a_lr = _sigmoid(a0_ref[...] + _dot(xa.astype(BF16), a2_ref[...]))
    g_ref[...] = _dot(_sigmoid(xg).astype(BF16), g2_ref[...])

    li = lax.broadcasted_iota(jnp.int32, (LANES, LANES), 0) // RWKV_HEAD
    lj = lax.broadcasted_iota(jnp.int32, (LANES, LANES), 1) // RWKV_HEAD
    e_head = jnp.where(li == lj, 1.0, 0.0).astype(BF16)
    kk = kraw * kk_ref[...]
    norm = jnp.sqrt(_head_sum(kk * kk, e_head))
    kk = kk / jnp.maximum(norm, 1e-12)
    k = kraw * (1.0 + (a_lr - 1.0) * ka_ref[...])
    bonus_ref[...] = _head_sum(r * k * rk_ref[...], e_head) * v

    ti = lax.broadcasted_iota(jnp.int32, (T, T), 0)
    tj = lax.broadcasted_iota(jnp.int32, (T, T), 1)
    tri = jnp.where((ti // CHUNK == tj // CHUNK) & (tj <= ti), 1.0, 0.0).astype(BF16)
    cum = _dot_exact_lhs(tri, logw)
    p_in = jnp.exp(cum)
    p_out = jnp.exp(-cum)
    rt_ref[...] = r * p_in
    at_ref[...] = -kk * jnp.exp(cum - logw)
    bt_ref[...] = kk * a_lr * p_out
    kt_ref[...] = k * p_out
    for c in range(T // CHUNK):
        pc_ref[c] = p_in[(c + 1) * CHUNK - 1:(c + 1) * CHUNK, :]


def _rwkv_prep(zr, mu, w0, w2, a0, a2, g2, k_k, k_a, r_k, vres):
    L = zr.shape[0]
    T = min(128, L)
    D = D_RWKV
    row = lambda i: (i, 0)
    const = lambda i: (0, 0)
    vec = lambda a: a.reshape(1, -1)
    has_vres = vres is not None
    ins = [zr, zr, vec(mu), vec(w0), w2, vec(a0), a2, g2, vec(k_k), vec(k_a), vec(r_k)]
    in_specs = [pl.BlockSpec((T, ZR_WIDTH), row),
                pl.BlockSpec((8, ZR_WIDTH), lambda i: (jnp.maximum(i * (T // 8) - 1, 0), 0)),
                pl.BlockSpec((1, ZR_WIDTH), const),
                pl.BlockSpec((1, D), const), pl.BlockSpec((LANES, D), const),
                pl.BlockSpec((1, D), const), pl.BlockSpec((LANES, D), const),
                pl.BlockSpec((LORA_GATE, D), const),
                pl.BlockSpec((1, D), const), pl.BlockSpec((1, D), const), pl.BlockSpec((1, D), const)]
    if has_vres:
        v_first, v_up, v_bias = vres
        ins += [v_first, v_up, vec(v_bias)]
        in_specs += [pl.BlockSpec((T, D), row), pl.BlockSpec((LANES, D), const),
                     pl.BlockSpec((1, D), const)]
    big = jax.ShapeDtypeStruct((L, D), F32)
    return pl.pallas_call(
        functools.partial(_rwkv_prep_kernel, T=T, has_vres=has_vres),
        grid=(L // T,),
        in_specs=in_specs,
        out_specs=[pl.BlockSpec((T, D), row)] * 7
                  + [pl.BlockSpec((T // CHUNK, 1, D), lambda i: (i, 0, 0))],
        out_shape=[big] * 7 + [jax.ShapeDtypeStruct((L // CHUNK, 1, D), F32)],
        compiler_params=_params(("parallel",)),
        name="rwkv_prep",
    )(*ins)


def _stack2(x, lo):
    zero = jnp.zeros_like(x)
    return jnp.concatenate([jnp.where(lo, x, zero), jnp.where(lo, zero, x)], axis=0)


def _scan_a_kernel(rt_ref, at_ref, bt_ref, kt_ref, v_ref, pc_ref, rh_ref, y0_ref, g_ref, h_ref,
                   *, CB):
    C = CHUNK
    lane = lax.broadcasted_iota(jnp.int32, (C, LANES), 1)
    lo = lane < RWKV_HEAD
    spos = lane % RWKV_HEAD
    tpos = lax.broadcasted_iota(jnp.int32, (C, LANES), 0)
    strict = spos < tpos
    incl = spos <= tpos
    eye_c = jnp.where(spos == tpos, 1.0, 0.0)
    ri = lax.broadcasted_iota(jnp.int32, (LANES, LANES), 0)
    ci = lax.broadcasted_iota(jnp.int32, (LANES, LANES), 1)
    same_head = (ri // RWKV_HEAD) == (ci // RWKV_HEAD)
    diag = ri == ci

    def stk(x):
        return _stack2(x, lo).astype(BF16)

    def body(c, carry):
        rows = pl.ds(pl.multiple_of(c * C, C), C)
        rt = rt_ref[rows, :]
        at = at_ref[rows, :]
        bt = bt_ref[rows, :]
        kt = kt_ref[rows, :]
        v = v_ref[rows, :]
        pc = pc_ref[c]
        at_b = at.astype(BF16)
        rt_b = rt.astype(BF16)
        bs = stk(bt)
        ks = stk(kt)
        vs = stk(v)
        nt = lambda a, b: lax.dot_general(a, b, NT_DIMS, preferred_element_type=F32)
        m_ab = jnp.where(strict, nt(at_b, bs), 0.0)
        m_ak = jnp.where(strict, nt(at_b, ks), 0.0)
        m_rb = jnp.where(incl, nt(rt_b, bs), 0.0)
        m_rk = jnp.where(incl, nt(rt_b, ks), 0.0)
        pw = m_ab
        inv = eye_c + pw
        for _ in range(5):
            pw = _dot(pw.astype(BF16), stk(pw))
            inv = inv + _dot(inv.astype(BF16), stk(pw))
        inv_b = inv.astype(BF16)
        w = _dot(inv_b, stk(at))
        u0 = _dot(inv_b, stk(_dot(m_ak.astype(BF16), vs)))
        m_rb_b = m_rb.astype(BF16)
        rh_ref[rows, :] = rt + _dot(m_rb_b, stk(w))
        y0_ref[rows, :] = _dot(m_rb_b, stk(u0)) + _dot(m_rk.astype(BF16), vs)
        tn = lambda a, b: lax.dot_general(a, b, TN_DIMS, preferred_element_type=F32)
        btp = (bt * pc).astype(BF16)
        ktp = (kt * pc).astype(BF16)
        w_b = w.astype(BF16)
        u0_b = u0.astype(BF16)
        g_ref[c, 0] = (jnp.where(diag, jnp.broadcast_to(pc, (LANES, LANES)), 0.0)
                       + jnp.where(same_head, tn(btp, w_b), 0.0))
        h_ref[c, 0] = jnp.where(same_head, tn(btp, u0_b) + tn(ktp, v.astype(BF16)), 0.0)
        return carry

    lax.fori_loop(0, CB, body, 0)


def _scan_a(rt, at, bt, kt, v, pc):
    L, D = rt.shape
    NP = D // LANES
    NC = L // CHUNK
    CB = min(8, NC)
    blk = pl.BlockSpec((CB * CHUNK, LANES), lambda ci, p: (ci, p))
    gh = pl.BlockSpec((CB, 1, LANES, LANES), lambda ci, p: (ci, p, 0, 0))
    return pl.pallas_call(
        functools.partial(_scan_a_kernel, CB=CB),
        grid=(NC // CB, NP),
        in_specs=[blk] * 5 + [pl.BlockSpec((CB, 1, LANES), lambda ci, p: (ci, 0, p))],
        out_specs=[blk, blk, gh, gh],
        out_shape=[jax.ShapeDtypeStruct((L, D), F32), jax.ShapeDtypeStruct((L, D), F32),
                   jax.ShapeDtypeStruct((NC, NP, LANES, LANES), F32),
                   jax.ShapeDtypeStruct((NC, NP, LANES, LANES), F32)],
        compiler_params=_params(("parallel", "parallel")),
        name="rwkv_chunk_local",
    )(rt, at, bt, kt, v, pc)


def _scan_b_kernel(rh_ref, y0_ref, g_ref, h_ref, bonus_ref, gate_ref, lg_ref, lb_ref, o_ref,
                   st_ref, *, CB, PP):
    ci = pl.program_id(1)

    @pl.when(ci == 0)
    def _():
        st_ref[...] = jnp.zeros(st_ref.shape, F32)

    li = lax.broadcasted_iota(jnp.int32, (LANES, LANES), 0) // RWKV_HEAD
    lj = lax.broadcasted_iota(jnp.int32, (LANES, LANES), 1) // RWKV_HEAD
    e_mean = jnp.where(li == lj, 1.0 / RWKV_HEAD, 0.0).astype(BF16)

    def body(c, carry):
        rows = pl.ds(pl.multiple_of(c * CHUNK, CHUNK), CHUNK)
        for p in range(PP):
            sl = slice(p * LANES, (p + 1) * LANES)
            st = st_ref[p]
            y = _dot3(rh_ref[rows, sl], st) + y0_ref[rows, sl]
            st_ref[p] = _dot3(g_ref[c, p], st) + h_ref[c, p]
            mean = _dot_exact_rhs(y, e_mean)
            d = y - mean
            var = _dot_exact_rhs(d * d, e_mean)
            yn = d * lax.rsqrt(var + RWKV_GN_EPS) * lg_ref[:, sl] + lb_ref[:, sl]
            o_ref[rows, sl] = ((yn + bonus_ref[rows, sl]) * gate_ref[rows, sl]).astype(o_ref.dtype)
        return carry

    lax.fori_loop(0, CB, body, 0)


def _scan_b(rh, y0, g, h, bonus, gate, ln_g, ln_b):
    L, D = rh.shape
    NP = D // LANES
    NC = L // CHUNK
    CB = min(8, NC)
    PP = 4
    blk = pl.BlockSpec((CB * CHUNK, PP * LANES), lambda pg, ci: (ci, pg))
    gh = pl.BlockSpec((CB, PP, LANES, LANES), lambda pg, ci: (ci, pg, 0, 0))
    vec = pl.BlockSpec((1, PP * LANES), lambda pg, ci: (0, pg))
    return pl.pallas_call(
        functools.partial(_scan_b_kernel, CB=CB, PP=PP),
        grid=(NP // PP, NC // CB),
        in_specs=[blk, blk, gh, gh, blk, blk, vec, vec],
        out_specs=blk,
        out_shape=jax.ShapeDtypeStruct((L, D), BF16),
        scratch_shapes=[pltpu.VMEM((PP, LANES, LANES), F32)],
        compiler_params=_params(("parallel", "arbitrary")),
        name="rwkv_state_scan",
    )(rh, y0, g, h, bonus, gate, ln_g.reshape(1, -1), ln_b.reshape(1, -1))


def _pad_cols(w, width):
    return jnp.pad(w, ((0, 0), (0, width - w.shape[1])))


def _pad_rows(w, height):
    return jnp.pad(w, ((0, height - w.shape[0]), (0, 0)))


def _pad_vec(v, width):
    return jnp.pad(v, (0, width - v.shape[0]))


def kernel(x, w_in, norm_mix, dw_weight, dw_bias, conv_ln_g, conv_ln_b, w_conv_out, w_att_out, rwkv_mu, rwkv_w0, rwkv_w2, rwkv_a0, rwkv_a2, rwkv_g2, rwkv_k_k, rwkv_k_a, rwkv_r_k, rwkv_ln_g, rwkv_ln_b, vres_down, vres_mu, vres_up, vres_bias, w_rwkv_out, w_out, norm_ffn, w_ffn_gate, w_ffn_up, w_ffn_down, norm_final):
    B, L, D = x.shape
    assert B == 1 and D == D_MODEL and L % 256 == 0
    depth = w_in.shape[0]
    xs = x.reshape(L, D)
    o_att = 2 * D_CONV
    o_idx = o_att + 3 * D_ATT
    o_rwkv = o_idx + IDX_HEADS * IDX_HEAD_DIM + IDX_HEAD_DIM + IDX_HEADS
    o_gate = o_rwkv + 3 * D_RWKV + LORA_DECAY + LORA_AAA + LORA_GATE
    v_first = None
    for l in range(depth):
        wl = w_in[l]
        w_b = jnp.concatenate([wl[:, :o_idx], wl[:, o_gate:]], axis=1).astype(BF16)
        r0 = o_rwkv + 3 * D_RWKV
        vdown = vres_down[l - 1] if l > 0 else jnp.zeros((D, LORA_MV), F32)
        w_r = jnp.concatenate([
            wl[:, o_rwkv:r0],
            _pad_cols(wl[:, r0:r0 + LORA_DECAY], LANES),
            _pad_cols(wl[:, r0 + LORA_DECAY:r0 + LORA_DECAY + LORA_AAA], LANES),
            wl[:, r0 + LORA_DECAY + LORA_AAA:o_gate],
            _pad_cols(vdown, ZR_WIDTH - ZR_XV)], axis=1).astype(BF16)
        i0 = o_idx + IDX_HEADS * IDX_HEAD_DIM
        w_ik = wl[:, i0:i0 + IDX_HEAD_DIM]
        w_i = jnp.concatenate([wl[:, o_idx:i0], w_ik, w_ik,
                               _pad_cols(wl[:, i0 + IDX_HEAD_DIM:o_rwkv], LANES)], axis=1).astype(BF16)
        mu = rwkv_mu[l]
        mu_p = jnp.concatenate([
            mu[:3 * D_RWKV],
            _pad_vec(mu[3 * D_RWKV:3 * D_RWKV + LORA_DECAY], LANES),
            _pad_vec(mu[3 * D_RWKV + LORA_DECAY:3 * D_RWKV + LORA_DECAY + LORA_AAA], LANES),
            mu[3 * D_RWKV + LORA_DECAY + LORA_AAA:],
            _pad_vec(vres_mu[l - 1] if l > 0 else jnp.zeros((LORA_MV,), F32), ZR_WIDTH - ZR_XV)])

        h = _rms_norm(xs, norm_mix[l], BF16)
        zb = _matmul(h, w_b, BF16, 1024, 512, "in_proj_bf16")
        zr = _matmul(h, w_r, F32, 1024, 768, "in_proj_rwkv")
        zi = _matmul(h, w_i, F32, 1024, 256, "in_proj_index")

        a_mix = _conformer_conv(zb, dw_weight[l], dw_bias[l], conv_ln_g[l], conv_ln_b[l])

        qr, kr, iqr, ik2, iws = _att_prep(zb, zi)
        b_mix = _dsa_attention(qr, iqr, iws, ik2, kr, zb)

        vres = None
        if l > 0:
            vres = (v_first, _pad_rows(vres_up[l - 1], LANES).astype(BF16), vres_bias[l - 1])
        rt, at, bt, kt, v_rwkv, bonus, gate, pc = _rwkv_prep(
            zr, mu_p, rwkv_w0[l], _pad_rows(rwkv_w2[l], LANES).astype(BF16), rwkv_a0[l],
            _pad_rows(rwkv_a2[l], LANES).astype(BF16), rwkv_g2[l].astype(BF16),
            rwkv_k_k[l], rwkv_k_a[l], rwkv_r_k[l].reshape(-1), vres)
        if l == 0:
            v_first = v_rwkv
        rh, y0, g_mat, h_mat = _scan_a(rt, at, bt, kt, v_rwkv, pc)
        c_mix = _scan_b(rh, y0, g_mat, h_mat, bonus, gate, rwkv_ln_g[l], rwkv_ln_b[l])

        merged = _merge(a_mix, b_mix, c_mix, w_conv_out[l].astype(BF16), w_att_out[l].astype(BF16),
                        w_rwkv_out[l].astype(BF16), zb, 1024, 512)
        xs = _matmul_residual(merged, w_out[l].astype(BF16), xs, 1024, 512, "out_proj")

        h2 = _rms_norm(xs, norm_ffn[l], BF16)
        act = _ffn1(h2, _pad_cols(w_ffn_gate[l], D_FF_PAD).astype(BF16),
                    _pad_cols(w_ffn_up[l], D_FF_PAD).astype(BF16), 1024, 512)
        xs = _matmul_residual(act, _pad_rows(w_ffn_down[l], D_FF_PAD).astype(BF16), xs, 512, 256,
                              "ffn_down")
    return _rms_norm(xs, norm_final, F32).reshape(B, L, D)
```

```python
import functools

import jax
import jax.numpy as jnp
import numpy as np
from jax import lax
from jax.experimental import pallas as pl
from jax.experimental.pallas import tpu as pltpu

F32 = jnp.float32
BF16 = jnp.bfloat16

D_MODEL = 4096
RMS_EPS = 1e-6
LN_EPS = 1e-5
D_CONV = D_MODEL // 4
CONV_WIDTH = 31
ATT_HEADS = 8
ATT_HEAD_DIM = 128
D_ATT = ATT_HEADS * ATT_HEAD_DIM
ATT_SCALE = ATT_HEAD_DIM ** -0.5
LOG2_E = 1.4426950408889634
IDX_HEADS = 16
IDX_HEAD_DIM = 64
IDX_SCALE = IDX_HEAD_DIM ** -0.5
IDX_W_SCALE = IDX_HEADS ** -0.5
TOPK_MAX = 256
ROPE_THETA = 500000.0
ROPE_FRACTION = 4
D_RWKV = D_MODEL // 2
RWKV_HEAD = 64
LORA_DECAY = 96
LORA_AAA = 96
LORA_MV = 64
LORA_GATE = 256
RWKV_GN_EPS = 64e-5
D_FF = ((8 * D_MODEL + 3 * 256 - 1) // (3 * 256)) * 256

LANES = 128
CHUNK = 64
D_FF_PAD = 11264
ZR_XW = 3 * D_RWKV
ZR_XA = ZR_XW + LANES
ZR_XG = ZR_XA + LANES
ZR_XV = ZR_XG + LORA_GATE
ZR_USED = ZR_XV + LANES
ZR_WIDTH = 6912
ZI_IK = IDX_HEADS * IDX_HEAD_DIM
ZI_IW = ZI_IK + LANES
ZI_WIDTH = ZI_IW + LANES
ZB_GATE = 2 * D_CONV + 3 * D_ATT
ZB_WIDTH = ZB_GATE + 3 * D_MODEL
VMEM_LIMIT = 56 * 1024 * 1024
INT_MIN = -2147483648
NEG_BIG = -1e30

NT_DIMS = (((1,), (1,)), ((), ()))
TN_DIMS = (((0,), (0,)), ((), ()))


def _params(sem, vmem=VMEM_LIMIT):
    return pltpu.CompilerParams(dimension_semantics=sem, vmem_limit_bytes=vmem)


def _dot(a, b):
    return jnp.dot(a, b, preferred_element_type=F32)


def _split2(x):
    hi = x.astype(BF16)
    lo = (x - hi.astype(F32)).astype(BF16)
    return hi, lo


def _dot_exact_rhs(x, e):
    h1 = x.astype(BF16)
    r1 = x - h1.astype(F32)
    h2 = r1.astype(BF16)
    h3 = (r1 - h2.astype(F32)).astype(BF16)
    return _dot(h1, e) + _dot(h2, e) + _dot(h3, e)


def _dot_exact_lhs(e, x):
    h1 = x.astype(BF16)
    r1 = x - h1.astype(F32)
    h2 = r1.astype(BF16)
    h3 = (r1 - h2.astype(F32)).astype(BF16)
    return _dot(e, h1) + _dot(e, h2) + _dot(e, h3)


def _dot3(a, b):
    ah, al = _split2(a)
    bh, bl = _split2(b)
    return _dot(ah, bh) + _dot(ah, bl) + _dot(al, bh)


def _sigmoid(x):
    return 1.0 / (1.0 + jnp.exp(-x))


def _rms_kernel(x_ref, g_ref, o_ref):
    x = x_ref[...]
    y = x * lax.rsqrt(jnp.mean(x * x, axis=-1, keepdims=True) + RMS_EPS)
    o_ref[...] = (y * g_ref[...]).astype(o_ref.dtype)


def _rms_norm(x, g, out_dtype):
    L, D = x.shape
    tr = min(256, L)
    return pl.pallas_call(
        _rms_kernel,
        grid=(L // tr,),
        in_specs=[pl.BlockSpec((tr, D), lambda i: (i, 0)),
                  pl.BlockSpec((1, D), lambda i: (0, 0))],
        out_specs=pl.BlockSpec((tr, D), lambda i: (i, 0)),
        out_shape=jax.ShapeDtypeStruct((L, D), out_dtype),
        compiler_params=_params(("parallel",)),
        name="rms_norm",
    )(x, g.reshape(1, D))


def _mm_kernel(a_ref, b_ref, o_ref):
    o_ref[...] = _dot(a_ref[...], b_ref[...]).astype(o_ref.dtype)


def _matmul(a, b, out_dtype, tm, tn, name):
    M, K = a.shape
    N = b.shape[1]
    tm = min(tm, M)
    return pl.pallas_call(
        _mm_kernel,
        grid=(M // tm, N // tn),
        in_specs=[pl.BlockSpec((tm, K), lambda i, j: (i, 0)),
                  pl.BlockSpec((K, tn), lambda i, j: (0, j))],
        out_specs=pl.BlockSpec((tm, tn), lambda i, j: (i, j)),
        out_shape=jax.ShapeDtypeStruct((M, N), out_dtype),
        compiler_params=_params(("parallel", "parallel")),
        name=name,
    )(a, b)


def _mm_res_kernel(a_ref, b_ref, x_ref, o_ref):
    o_ref[...] = x_ref[...] + _dot(a_ref[...], b_ref[...])


def _matmul_residual(a, b, x, tm, tn, name):
    M, K = a.shape
    N = b.shape[1]
    tm = min(tm, M)
    return pl.pallas_call(
        _mm_res_kernel,
        grid=(M // tm, N // tn),
        in_specs=[pl.BlockSpec((tm, K), lambda i, j: (i, 0)),
                  pl.BlockSpec((K, tn), lambda i, j: (0, j)),
                  pl.BlockSpec((tm, tn), lambda i, j: (i, j))],
        out_specs=pl.BlockSpec((tm, tn), lambda i, j: (i, j)),
        out_shape=jax.ShapeDtypeStruct((M, N), F32),
        compiler_params=_params(("parallel", "parallel")),
        name=name,
    )(a, b, x)


def _ffn1_kernel(h_ref, wg_ref, wu_ref, o_ref):
    h = h_ref[...]
    g = _dot(h, wg_ref[...])
    u = _dot(h, wu_ref[...])
    o_ref[...] = (g * _sigmoid(g) * u).astype(o_ref.dtype)


def _ffn1(h, wg, wu, tm, tn):
    M, K = h.shape
    N = wg.shape[1]
    tm = min(tm, M)
    return pl.pallas_call(
        _ffn1_kernel,
        grid=(M // tm, N // tn),
        in_specs=[pl.BlockSpec((tm, K), lambda i, j: (i, 0)),
                  pl.BlockSpec((K, tn), lambda i, j: (0, j)),
                  pl.BlockSpec((K, tn), lambda i, j: (0, j))],
        out_specs=pl.BlockSpec((tm, tn), lambda i, j: (i, j)),
        out_shape=jax.ShapeDtypeStruct((M, N), BF16),
        compiler_params=_params(("parallel", "parallel")),
        name="ffn_gate_up",
    )(h, wg, wu)


def _merge_kernel(a_ref, b_ref, c_ref, wa_ref, wb_ref, wc_ref, g0_ref, g1_ref, g2_ref, o_ref):
    ya = _dot(a_ref[...], wa_ref[...])
    yb = _dot(b_ref[...], wb_ref[...])
    yc = _dot(c_ref[...], wc_ref[...])
    m = (_sigmoid(g0_ref[...].astype(F32)) * ya + _sigmoid(g1_ref[...].astype(F32)) * yb
         + _sigmoid(g2_ref[...].astype(F32)) * yc)
    o_ref[...] = m.astype(o_ref.dtype)


def _merge(a, b, c, wa, wb, wc, zb, tm, tn):
    M = a.shape[0]
    tm = min(tm, M)
    g_off = ZB_GATE // tn
    g_stride = D_MODEL // tn
    return pl.pallas_call(
        _merge_kernel,
        grid=(M // tm, D_MODEL // tn),
        in_specs=[pl.BlockSpec((tm, D_CONV), lambda i, j: (i, 0)),
                  pl.BlockSpec((tm, D_ATT), lambda i, j: (i, 0)),
                  pl.BlockSpec((tm, D_RWKV), lambda i, j: (i, 0)),
                  pl.BlockSpec((D_CONV, tn), lambda i, j: (0, j)),
                  pl.BlockSpec((D_ATT, tn), lambda i, j: (0, j)),
                  pl.BlockSpec((D_RWKV, tn), lambda i, j: (0, j)),
                  pl.BlockSpec((tm, tn), lambda i, j: (i, g_off + j)),
                  pl.BlockSpec((tm, tn), lambda i, j: (i, g_off + g_stride + j)),
                  pl.BlockSpec((tm, tn), lambda i, j: (i, g_off + 2 * g_stride + j))],
        out_specs=pl.BlockSpec((tm, tn), lambda i, j: (i, j)),
        out_shape=jax.ShapeDtypeStruct((M, D_MODEL), BF16),
        compiler_params=_params(("parallel", "parallel")),
        name="gated_merge",
    )(a, b, c, wa, wb, wc, zb, zb, zb)


def _conv_kernel(val_ref, gate_ref, hval_ref, hgate_ref, w_ref, b_ref, lg_ref, lb_ref, o_ref,
                 c_ref, *, T):
    i = pl.program_id(0)
    halo = 32
    hv = hval_ref[...].astype(F32)
    hg = hgate_ref[...].astype(F32)
    hc = hv * _sigmoid(hg)
    c_ref[0:halo, :] = jnp.where(i > 0, hc, 0.0)
    v = val_ref[...].astype(F32)
    g = gate_ref[...].astype(F32)
    c_ref[halo:halo + T, :] = v * _sigmoid(g)
    acc = jnp.zeros((T, D_CONV), F32) + b_ref[...]
    for j in range(CONV_WIDTH):
        off = halo - (CONV_WIDTH - 1) + j
        acc = acc + c_ref[off:off + T, :] * w_ref[j:j + 1, :]
    m = jnp.mean(acc, axis=-1, keepdims=True)
    d = acc - m
    var = jnp.mean(d * d, axis=-1, keepdims=True)
    y = d * lax.rsqrt(var + LN_EPS) * lg_ref[...] + lb_ref[...]
    o_ref[...] = (y * _sigmoid(y)).astype(o_ref.dtype)


def _conformer_conv(zb, dw_w, dw_b, ln_g, ln_b):
    L = zb.shape[0]
    T = min(256, L)
    hb = T // 32
    w_pad = jnp.zeros((32, D_CONV), F32).at[:CONV_WIDTH].set(dw_w)
    return pl.pallas_call(
        functools.partial(_conv_kernel, T=T),
        grid=(L // T,),
        in_specs=[pl.BlockSpec((T, D_CONV), lambda i: (i, 0)),
                  pl.BlockSpec((T, D_CONV), lambda i: (i, 1)),
                  pl.BlockSpec((32, D_CONV), lambda i: (jnp.maximum(i * hb - 1, 0), 0)),
                  pl.BlockSpec((32, D_CONV), lambda i: (jnp.maximum(i * hb - 1, 0), 1)),
                  pl.BlockSpec((32, D_CONV), lambda i: (0, 0)),
                  pl.BlockSpec((1, D_CONV), lambda i: (0, 0)),
                  pl.BlockSpec((1, D_CONV), lambda i: (0, 0)),
                  pl.BlockSpec((1, D_CONV), lambda i: (0, 0))],
        out_specs=pl.BlockSpec((T, D_CONV), lambda i: (i, 0)),
        out_shape=jax.ShapeDtypeStruct((L, D_CONV), BF16),
        scratch_shapes=[pltpu.VMEM((T + 32, D_CONV), F32)],
        compiler_params=_params(("parallel",)),
        name="conformer_conv",
    )(zb, zb, zb, zb, w_pad, dw_b.reshape(1, -1), ln_g.reshape(1, -1), ln_b.reshape(1, -1))


def _rope_tables(pos, inv_freq, head_dim):
    rd = head_dim // ROPE_FRACTION
    half = rd // 2
    ang = pos * inv_freq
    cos = jnp.cos(ang)
    sin = jnp.sin(ang)
    lane = lax.broadcasted_iota(jnp.int32, ang.shape, 1) % head_dim
    c = jnp.where(lane < rd, cos, 1.0)
    s_up = jnp.where(lane < half, -sin, 0.0)
    s_dn = jnp.where((lane >= half) & (lane < rd), sin, 0.0)
    return c, s_up, s_dn


def _rope_apply(x, tabs, half):
    c, s_up, s_dn = tabs
    return (x * c + pltpu.roll(x, LANES - half, 1) * s_up + pltpu.roll(x, half, 1) * s_dn)


def _attprep_kernel(q_ref, k_ref, zi_ref, fq_ref, fi_ref, qo_ref, ko_ref, iqo_ref, iko_ref,
                    iwo_ref, *, T):
    i = pl.program_id(0)
    pos = (i * T + lax.broadcasted_iota(jnp.int32, (T, LANES), 0)).astype(F32)
    tq = _rope_tables(pos, fq_ref[...], ATT_HEAD_DIM)
    ti = _rope_tables(pos, fi_ref[...], IDX_HEAD_DIM)
    hq = ATT_HEAD_DIM // ROPE_FRACTION // 2
    hi = IDX_HEAD_DIM // ROPE_FRACTION // 2
    for h in range(ATT_HEADS):
        sl = slice(h * LANES, (h + 1) * LANES)
        qo_ref[:, sl] = (_rope_apply(q_ref[:, sl].astype(F32), tq, hq)
                         * (ATT_SCALE * LOG2_E)).astype(BF16)
        ko_ref[:, sl] = _rope_apply(k_ref[:, sl].astype(F32), tq, hq).astype(BF16)
    for p in range(ZI_IK // LANES):
        sl = slice(p * LANES, (p + 1) * LANES)
        iqo_ref[:, sl] = _rope_apply(zi_ref[:, sl], ti, hi).astype(BF16)
    iko_ref[...] = _rope_apply(zi_ref[:, ZI_IK:ZI_IW], ti, hi).astype(BF16)
    iwo_ref[...] = zi_ref[:, ZI_IW:ZI_WIDTH] * (IDX_W_SCALE * IDX_SCALE)


def _lane_inv_freq(head_dim):
    rd = head_dim // ROPE_FRACTION
    half = rd // 2
    inv_freq = jnp.power(ROPE_THETA, -jnp.arange(half, dtype=F32) * (2.0 / rd))
    lane = np.arange(LANES) % head_dim
    idx = np.where(lane < rd, lane % half, 0)
    return jnp.where(jnp.asarray(lane < rd), inv_freq[idx], 0.0).reshape(1, LANES)


def _att_prep(zb, zi):
    L = zb.shape[0]
    T = min(256, L)
    row = lambda i: (i, 0)
    return pl.pallas_call(
        functools.partial(_attprep_kernel, T=T),
        grid=(L // T,),
        in_specs=[pl.BlockSpec((T, D_ATT), lambda i: (i, 2)),
                  pl.BlockSpec((T, D_ATT), lambda i: (i, 3)),
                  pl.BlockSpec((T, ZI_WIDTH), row),
                  pl.BlockSpec((1, LANES), lambda i: (0, 0)),
                  pl.BlockSpec((1, LANES), lambda i: (0, 0))],
        out_specs=[pl.BlockSpec((T, D_ATT), row), pl.BlockSpec((T, D_ATT), row),
                   pl.BlockSpec((T, ZI_IK), row), pl.BlockSpec((T, LANES), row),
                   pl.BlockSpec((T, LANES), row)],
        out_shape=[jax.ShapeDtypeStruct((L, D_ATT), BF16), jax.ShapeDtypeStruct((L, D_ATT), BF16),
                   jax.ShapeDtypeStruct((L, ZI_IK), BF16), jax.ShapeDtypeStruct((L, LANES), BF16),
                   jax.ShapeDtypeStruct((L, LANES), F32)],
        compiler_params=_params(("parallel",)),
        name="att_prep",
    )(zb, zb, zi, _lane_inv_freq(ATT_HEAD_DIM), _lane_inv_freq(IDX_HEAD_DIM))


def _dsa_kernel(q_ref, iq_ref, iw_ref, ik_ref, k_ref, v_ref, o_ref,
                key_ref, wb_ref, iqm_ref, thr_ref, m_ref, l_ref, acc_ref, *, Q, S, RB, topk):
    i = pl.program_id(0)
    j = pl.program_id(1)
    q_end = (i + 1) * Q

    @pl.when(j == 0)
    def _select():
        lane = lax.broadcasted_iota(jnp.int32, (Q, LANES), 1)
        lo = lane < IDX_HEAD_DIM
        zero = jnp.zeros((Q, LANES), BF16)
        for p in range(IDX_HEADS // 2):
            x = iq_ref[:, p * LANES:(p + 1) * LANES]
            iqm_ref[2 * p] = jnp.where(lo, x, zero)
            iqm_ref[2 * p + 1] = jnp.where(lo, zero, x)
        w = iw_ref[...]
        for h in range(IDX_HEADS):
            wb_ref[h] = jnp.broadcast_to(w[:, h:h + 1], (Q, LANES))
        row = i * Q + lax.broadcasted_iota(jnp.int32, (Q, S), 0)

        def score_chunk(c, carry):
            off = pl.multiple_of(c * S, S)
            ikc = ik_ref[pl.ds(off, S), :]
            sc = jnp.zeros((Q, S), F32)
            for h in range(IDX_HEADS):
                lg = lax.dot_general(iqm_ref[h], ikc, NT_DIMS, preferred_element_type=F32)
                sc = sc + jnp.maximum(lg, 0.0) * jnp.tile(wb_ref[h], (1, S // LANES))
            sc = jnp.where(sc == 0.0, 0.0, sc)
            bits = pltpu.bitcast(sc, jnp.int32)
            key = bits ^ ((bits >> 31) & 0x7FFFFFFF)
            col = off + lax.broadcasted_iota(jnp.int32, (Q, S), 1)
            key_ref[:, pl.ds(off, S)] = jnp.where(col <= row, key, INT_MIN)
            return carry

        nchunk = (q_end + S - 1) // S
        lax.fori_loop(0, nchunk, score_chunk, 0)

        def row_block(rb, carry):
            rows = pl.ds(pl.multiple_of(rb * RB, RB), RB)

            def bit_step(b, prefix):
                cand_u = prefix | jnp.left_shift(jnp.int32(1), 31 - b)
                cand = jnp.broadcast_to(cand_u ^ INT_MIN, (RB, LANES))

                def count(c, cnt):
                    off = pl.multiple_of(c * S, S)
                    for u in range(S // LANES):
                        kc = key_ref[rows, pl.ds(off + u * LANES, LANES)]
                        cnt = cnt + jnp.where(kc >= cand, 1.0, 0.0)
                    return cnt

                cnt = lax.fori_loop(0, nchunk, count, jnp.zeros((RB, LANES), F32))
                tot = jnp.sum(cnt, axis=1, keepdims=True)
                return jnp.where(tot >= float(topk), cand_u, prefix)

            prefix = lax.fori_loop(0, 32, bit_step, jnp.zeros((RB, 1), jnp.int32))
            thr = jnp.maximum(prefix ^ INT_MIN, INT_MIN + 1)
            thr_ref[rows, :] = jnp.broadcast_to(thr, (RB, LANES))
            return carry

        lax.fori_loop(0, Q // RB, row_block, 0)
        m_ref[...] = jnp.full(m_ref.shape, NEG_BIG, F32)
        l_ref[...] = jnp.zeros(l_ref.shape, F32)
        acc_ref[...] = jnp.zeros(acc_ref.shape, F32)

    @pl.when(j * S < q_end)
    def _attend():
        keys = key_ref[:, pl.ds(pl.multiple_of(j * S, S), S)]
        bias = jnp.where(keys >= jnp.tile(thr_ref[...], (1, S // LANES)), 0.0, NEG_BIG)
        for h in range(ATT_HEADS):
            sl = slice(h * LANES, (h + 1) * LANES)
            s = lax.dot_general(q_ref[:, sl], k_ref[:, sl], NT_DIMS, preferred_element_type=F32)
            s = s + bias
            m_prev = m_ref[h]
            m_new = jnp.maximum(m_prev, jnp.max(s, axis=1, keepdims=True))
            alpha = jnp.exp2(m_prev - m_new)
            p = jnp.exp2(s - jnp.tile(m_new, (1, S // LANES)))
            l_ref[h] = alpha * l_ref[h] + jnp.sum(p, axis=1, keepdims=True)
            acc_ref[:, sl] = alpha * acc_ref[:, sl] + _dot(p.astype(BF16), v_ref[:, sl])
            m_ref[h] = m_new

    @pl.when(j == pl.num_programs(1) - 1)
    def _finish():
        for h in range(ATT_HEADS):
            sl = slice(h * LANES, (h + 1) * LANES)
            o_ref[:, sl] = (acc_ref[:, sl] / l_ref[h]).astype(o_ref.dtype)


def _dsa_attention(qr, iqr, iws, ik2, kr, zb):
    L = qr.shape[0]
    Q = min(256, L)
    S = min(512, L)
    RB = 128
    topk = min(TOPK_MAX, L // 4)
    nk = L // S

    def kv_block(i, j):
        return jnp.minimum(j, ((i + 1) * Q - 1) // S)

    return pl.pallas_call(
        functools.partial(_dsa_kernel, Q=Q, S=S, RB=RB, topk=topk),
        grid=(L // Q, nk),
        in_specs=[pl.BlockSpec((Q, D_ATT), lambda i, j: (i, 0)),
                  pl.BlockSpec((Q, ZI_IK), lambda i, j: (i, 0)),
                  pl.BlockSpec((Q, LANES), lambda i, j: (i, 0)),
                  pl.BlockSpec((L, LANES), lambda i, j: (0, 0)),
                  pl.BlockSpec((S, D_ATT), lambda i, j: (kv_block(i, j), 0)),
                  pl.BlockSpec((S, D_ATT), lambda i, j: (kv_block(i, j), 4))],
        out_specs=pl.BlockSpec((Q, D_ATT), lambda i, j: (i, 0)),
        out_shape=jax.ShapeDtypeStruct((L, D_ATT), BF16),
        scratch_shapes=[pltpu.VMEM((Q, L), jnp.int32),
                        pltpu.VMEM((IDX_HEADS, Q, LANES), F32),
                        pltpu.VMEM((IDX_HEADS, Q, LANES), BF16),
                        pltpu.VMEM((Q, LANES), jnp.int32),
                        pltpu.VMEM((ATT_HEADS, Q, LANES), F32),
                        pltpu.VMEM((ATT_HEADS, Q, LANES), F32),
                        pltpu.VMEM((Q, D_ATT), F32)],
        compiler_params=_params(("parallel", "arbitrary")),
        name="dsa_attention",
    )(qr, iqr, iws, ik2, kr, zb)


def _head_sum(x, e):
    cols = []
    for p in range(x.shape[1] // LANES):
        cols.append(_dot_exact_rhs(x[:, p * LANES:(p + 1) * LANES], e))
    return jnp.concatenate(cols, axis=1)


def _rwkv_prep_kernel(*refs, T, has_vres):
    if has_vres:
        (z_ref, halo_ref, mu_ref, w0_ref, w2_ref, a0_ref, a2_ref, g2_ref, kk_ref, ka_ref, rk_ref,
         vf_ref, vup_ref, vb_ref,
         rt_ref, at_ref, bt_ref, kt_ref, v_ref, bonus_ref, g_ref, pc_ref) = refs
    else:
        (z_ref, halo_ref, mu_ref, w0_ref, w2_ref, a0_ref, a2_ref, g2_ref, kk_ref, ka_ref, rk_ref,
         rt_ref, at_ref, bt_ref, kt_ref, v_ref, bonus_ref, g_ref, pc_ref) = refs
    i = pl.program_id(0)
    D = D_RWKV

    def shifted(lo, hi):
        z = z_ref[:, lo:hi]
        first = jnp.where(i > 0, halo_ref[7:8, lo:hi], 0.0)
        rows = lax.broadcasted_iota(jnp.int32, z.shape, 0)
        prev = jnp.where(rows == 0, first, pltpu.roll(z, 1, 0))
        return z + (prev - z) * mu_ref[:, lo:hi]

    r = shifted(0, D)
    kraw = shifted(D, 2 * D)
    v = shifted(2 * D, 3 * D)
    xw = shifted(ZR_XW, ZR_XA)
    xa = shifted(ZR_XA, ZR_XG)
    xg = shifted(ZR_XG, ZR_XV)
    if has_vres:
        xv = shifted(ZR_XV, ZR_USED)
        mix = _sigmoid(vb_ref[...] + _dot(xv.astype(BF16), vup_ref[...]))
        v = v + (vf_ref[...] - v) * mix
    v_ref[...] = v

    w_in = w0_ref[...] + _dot(jnp.tanh(xw).astype(BF16), w2_ref[...])
    w_log = -(jnp.maximum(-w_in, 0.0) + jnp.log(1.0 + jnp.exp(-jnp.abs(w_in)))) - 0.5
    logw = -jnp.exp(w_log)
    a_lr = _sigmoid(a0_ref[...] + _dot(xa.astype(BF16), a2_ref[...]))
    g_ref[...] = _dot(_sigmoid(xg).astype(BF16), g2_ref[...])

    li = lax.broadcasted_iota(jnp.int32, (LANES, LANES), 0) // RWKV_HEAD
    lj = lax.broadcasted_iota(jnp.int32, (LANES, LANES), 1) // RWKV_HEAD
    e_head = jnp.where(li == lj, 1.0, 0.0).astype(BF16)
    kk = kraw * kk_ref[...]
    norm = jnp.sqrt(_head_sum(kk * kk, e_head))
    kk = kk / jnp.maximum(norm, 1e-12)
    k = kraw * (1.0 + (a_lr - 1.0) * ka_ref[...])
    bonus_ref[...] = _head_sum(r * k * rk_ref[...], e_head) * v

    ti = lax.broadcasted_iota(jnp.int32, (T, T), 0)
    tj = lax.broadcasted_iota(jnp.int32, (T, T), 1)
    tri = jnp.where((ti // CHUNK == tj // CHUNK) & (tj <= ti), 1.0, 0.0).astype(BF16)
    cum = _dot_exact_lhs(tri, logw)
    p_in = jnp.exp(cum)
    p_out = jnp.exp(-cum)
    rt_ref[...] = r * p_in
    at_ref[...] = -kk * jnp.exp(cum - logw)
    bt_ref[...] = kk * a_lr * p_out
    kt_ref[...] = k * p_out
    for c in range(T // CHUNK):
        pc_ref[c] = p_in[(c + 1) * CHUNK - 1:(c + 1) * CHUNK, :]


def _rwkv_prep(zr, mu, w0, w2, a0, a2, g2, k_k, k_a, r_k, vres):
    L = zr.shape[0]
    T = min(128, L)
    D = D_RWKV
    row = lambda i: (i, 0)
    const = lambda i: (0, 0)
    vec = lambda a: a.reshape(1, -1)
    has_vres = vres is not None
    ins = [zr, zr, vec(mu), vec(w0), w2, vec(a0), a2, g2, vec(k_k), vec(k_a), vec(r_k)]
    in_specs = [pl.BlockSpec((T, ZR_WIDTH), row),
                pl.BlockSpec((8, ZR_WIDTH), lambda i: (jnp.maximum(i * (T // 8) - 1, 0), 0)),
                pl.BlockSpec((1, ZR_WIDTH), const),
                pl.BlockSpec((1, D), const), pl.BlockSpec((LANES, D), const),
                pl.BlockSpec((1, D), const), pl.BlockSpec((LANES, D), const),
                pl.BlockSpec((LORA_GATE, D), const),
                pl.BlockSpec((1, D), const), pl.BlockSpec((1, D), const), pl.BlockSpec((1, D), const)]
    if has_vres:
        v_first, v_up, v_bias = vres
        ins += [v_first, v_up, vec(v_bias)]
        in_specs += [pl.BlockSpec((T, D), row), pl.BlockSpec((LANES, D), const),
                     pl.BlockSpec((1, D), const)]
    big = jax.ShapeDtypeStruct((L, D), F32)
    return pl.pallas_call(
        functools.partial(_rwkv_prep_kernel, T=T, has_vres=has_vres),
        grid=(L // T,),
        in_specs=in_specs,
        out_specs=[pl.BlockSpec((T, D), row)] * 7
                  + [pl.BlockSpec((T // CHUNK, 1, D), lambda i: (i, 0, 0))],
        out_shape=[big] * 7 + [jax.ShapeDtypeStruct((L // CHUNK, 1, D), F32)],
        compiler_params=_params(("parallel",)),
        name="rwkv_prep",
    )(*ins)


def _stack2(x, lo):
    zero = jnp.zeros_like(x)
    return jnp.concatenate([jnp.where(lo, x, zero), jnp.where(lo, zero, x)], axis=0)


def _scan_a_kernel(rt_ref, at_ref, bt_ref, kt_ref, v_ref, pc_ref, rh_ref, y0_ref, g_ref, h_ref,
                   *, CB, UNROLL):
    C = CHUNK
    lane = lax.broadcasted_iota(jnp.int32, (C, LANES), 1)
    lo = lane < RWKV_HEAD
    spos = lane % RWKV_HEAD
    tpos = lax.broadcasted_iota(jnp.int32, (C, LANES), 0)
    strict = spos < tpos
    incl = spos <= tpos
    eye_c = jnp.where(spos == tpos, 1.0, 0.0)
    ri = lax.broadcasted_iota(jnp.int32, (LANES, LANES), 0)
    ci = lax.broadcasted_iota(jnp.int32, (LANES, LANES), 1)
    same_head = (ri // RWKV_HEAD) == (ci // RWKV_HEAD)
    diag = ri == ci

    def stk(x):
        return _stack2(x, lo).astype(BF16)

    nt = lambda a, b: lax.dot_general(a, b, NT_DIMS, preferred_element_type=F32)
    tn = lambda a, b: lax.dot_general(a, b, TN_DIMS, preferred_element_type=F32)

    def each(fn, *cols):
        return [fn(*args) for args in zip(*cols)]

    def body(cg, carry):
        cs = [cg * UNROLL + u for u in range(UNROLL)]
        rows = [pl.ds(pl.multiple_of(c * C, C), C) for c in cs]
        rt = [rt_ref[r, :] for r in rows]
        at = [at_ref[r, :] for r in rows]
        bt = [bt_ref[r, :] for r in rows]
        kt = [kt_ref[r, :] for r in rows]
        v = [v_ref[r, :] for r in rows]
        pc = [pc_ref[c] for c in cs]
        at_b = each(lambda x: x.astype(BF16), at)
        rt_b = each(lambda x: x.astype(BF16), rt)
        bs = each(stk, bt)
        ks = each(stk, kt)
        vs = each(stk, v)
        m_ab = each(lambda a, b: jnp.where(strict, nt(a, b), 0.0), at_b, bs)
        m_ak = each(lambda a, b: jnp.where(strict, nt(a, b), 0.0), at_b, ks)
        m_rb = each(lambda a, b: jnp.where(incl, nt(a, b), 0.0), rt_b, bs)
        m_rk = each(lambda a, b: jnp.where(incl, nt(a, b), 0.0), rt_b, ks)
        mv = each(lambda m, x: _dot(m.astype(BF16), x), m_ak, vs)
        pw = m_ab
        inv = each(lambda m: eye_c + m, pw)
        for _ in range(5):
            pw = each(lambda x: _dot(x.astype(BF16), stk(x)), pw)
            inv = each(lambda t, x: t + _dot(t.astype(BF16), stk(x)), inv, pw)
        inv_b = each(lambda x: x.astype(BF16), inv)
        w = each(lambda t, x: _dot(t, stk(x)), inv_b, at)
        u0 = each(lambda t, x: _dot(t, stk(x)), inv_b, mv)
        m_rb_b = each(lambda x: x.astype(BF16), m_rb)
        rh = each(lambda r, m, x: r + _dot(m, stk(x)), rt, m_rb_b, w)
        y0 = each(lambda m, x, m2, x2: _dot(m, stk(x)) + _dot(m2.astype(BF16), x2),
                  m_rb_b, u0, m_rk, vs)
        btp = each(lambda x, p: (x * p).astype(BF16), bt, pc)
        ktp = each(lambda x, p: (x * p).astype(BF16), kt, pc)
        gm = each(lambda b, x, p: jnp.where(diag, jnp.broadcast_to(p, (LANES, LANES)), 0.0)
                  + jnp.where(same_head, tn(b, x.astype(BF16)), 0.0), btp, w, pc)
        hm = each(lambda b, x, k, y: jnp.where(same_head, tn(b, x.astype(BF16))
                                               + tn(k, y.astype(BF16)), 0.0), btp, u0, ktp, v)
        for u in range(UNROLL):
            rh_ref[rows[u], :] = rh[u]
            y0_ref[rows[u], :] = y0[u]
            g_ref[cs[u], 0] = gm[u]
            h_ref[cs[u], 0] = hm[u]
        return carry

    lax.fori_loop(0, CB // UNROLL, body, 0)


def _scan_a(rt, at, bt, kt, v, pc):
    L, D = rt.shape
    NP = D // LANES
    NC = L // CHUNK
    CB = min(8, NC)
    blk = pl.BlockSpec((CB * CHUNK, LANES), lambda ci, p: (ci, p))
    gh = pl.BlockSpec((CB, 1, LANES, LANES), lambda ci, p: (ci, p, 0, 0))
    return pl.pallas_call(
        functools.partial(_scan_a_kernel, CB=CB, UNROLL=min(8, CB)),
        grid=(NC // CB, NP),
        in_specs=[blk] * 5 + [pl.BlockSpec((CB, 1, LANES), lambda ci, p: (ci, 0, p))],
        out_specs=[blk, blk, gh, gh],
        out_shape=[jax.ShapeDtypeStruct((L, D), F32), jax.ShapeDtypeStruct((L, D), F32),
                   jax.ShapeDtypeStruct((NC, NP, LANES, LANES), F32),
                   jax.ShapeDtypeStruct((NC, NP, LANES, LANES), F32)],
        compiler_params=_params(("parallel", "parallel")),
        name="rwkv_chunk_local",
    )(rt, at, bt, kt, v, pc)


def _scan_b_kernel(rh_ref, y0_ref, g_ref, h_ref, bonus_ref, gate_ref, lg_ref, lb_ref, o_ref,
                   st_ref, *, CB, PP):
    ci = pl.program_id(1)

    @pl.when(ci == 0)
    def _():
        st_ref[...] = jnp.zeros(st_ref.shape, F32)

    li = lax.broadcasted_iota(jnp.int32, (LANES, LANES), 0) // RWKV_HEAD
    lj = lax.broadcasted_iota(jnp.int32, (LANES, LANES), 1) // RWKV_HEAD
    e_mean = jnp.where(li == lj, 1.0 / RWKV_HEAD, 0.0).astype(BF16)

    def body(c, carry):
        rows = pl.ds(pl.multiple_of(c * CHUNK, CHUNK), CHUNK)
        sls = [slice(p * LANES, (p + 1) * LANES) for p in range(PP)]

        def each(fn, *cols):
            return [fn(*args) for args in zip(*cols)]

        st = [st_ref[p] for p in range(PP)]
        y = each(lambda sl, s: _dot3(rh_ref[rows, sl], s) + y0_ref[rows, sl], sls, st)
        st_new = each(lambda p, s: _dot3(g_ref[c, p], s) + h_ref[c, p], list(range(PP)), st)
        for p in range(PP):
            st_ref[p] = st_new[p]
        mean = each(lambda x: _dot_exact_rhs(x, e_mean), y)
        d = each(lambda x, m: x - m, y, mean)
        var = each(lambda x: _dot_exact_rhs(x * x, e_mean), d)
        for p in range(PP):
            sl = sls[p]
            yn = d[p] * lax.rsqrt(var[p] + RWKV_GN_EPS) * lg_ref[:, sl] + lb_ref[:, sl]
            o_ref[rows, sl] = ((yn + bonus_ref[rows, sl]) * gate_ref[rows, sl]).astype(o_ref.dtype)
        return carry

    lax.fori_loop(0, CB, body, 0)


def _scan_b(rh, y0, g, h, bonus, gate, ln_g, ln_b):
    L, D = rh.shape
    NP = D // LANES
    NC = L // CHUNK
    CB = min(8, NC)
    PP = 8
    blk = pl.BlockSpec((CB * CHUNK, PP * LANES), lambda pg, ci: (ci, pg))
    gh = pl.BlockSpec((CB, PP, LANES, LANES), lambda pg, ci: (ci, pg, 0, 0))
    vec = pl.BlockSpec((1, PP * LANES), lambda pg, ci: (0, pg))
    return pl.pallas_call(
        functools.partial(_scan_b_kernel, CB=CB, PP=PP),
        grid=(NP // PP, NC // CB),
        in_specs=[blk, blk, gh, gh, blk, blk, vec, vec],
        out_specs=blk,
        out_shape=jax.ShapeDtypeStruct((L, D), BF16),
        scratch_shapes=[pltpu.VMEM((PP, LANES, LANES), F32)],
        compiler_params=_params(("parallel", "arbitrary")),
        name="rwkv_state_scan",
    )(rh, y0, g, h, bonus, gate, ln_g.reshape(1, -1), ln_b.reshape(1, -1))


def _pad_cols(w, width):
    return jnp.pad(w, ((0, 0), (0, width - w.shape[1])))


def _pad_rows(w, height):
    return jnp.pad(w, ((0, height - w.shape[0]), (0, 0)))


def _pad_vec(v, width):
    return jnp.pad(v, (0, width - v.shape[0]))


def kernel(x, w_in, norm_mix, dw_weight, dw_bias, conv_ln_g, conv_ln_b, w_conv_out, w_att_out, rwkv_mu, rwkv_w0, rwkv_w2, rwkv_a0, rwkv_a2, rwkv_g2, rwkv_k_k, rwkv_k_a, rwkv_r_k, rwkv_ln_g, rwkv_ln_b, vres_down, vres_mu, vres_up, vres_bias, w_rwkv_out, w_out, norm_ffn, w_ffn_gate, w_ffn_up, w_ffn_down, norm_final):
    B, L, D = x.shape
    assert B == 1 and D == D_MODEL and L % 256 == 0
    depth = w_in.shape[0]
    xs = x.reshape(L, D)
    o_att = 2 * D_CONV
    o_idx = o_att + 3 * D_ATT
    o_rwkv = o_idx + IDX_HEADS * IDX_HEAD_DIM + IDX_HEAD_DIM + IDX_HEADS
    o_gate = o_rwkv + 3 * D_RWKV + LORA_DECAY + LORA_AAA + LORA_GATE
    v_first = None
    for l in range(depth):
        wl = w_in[l].astype(BF16)
        w_b = jnp.concatenate([wl[:, :o_idx], wl[:, o_gate:]], axis=1)
        r0 = o_rwkv + 3 * D_RWKV
        vdown = (vres_down[l - 1] if l > 0 else jnp.zeros((D, LORA_MV), F32)).astype(BF16)
        w_r = jnp.concatenate([
            wl[:, o_rwkv:r0],
            _pad_cols(wl[:, r0:r0 + LORA_DECAY], LANES),
            _pad_cols(wl[:, r0 + LORA_DECAY:r0 + LORA_DECAY + LORA_AAA], LANES),
            wl[:, r0 + LORA_DECAY + LORA_AAA:o_gate],
            _pad_cols(vdown, ZR_WIDTH - ZR_XV)], axis=1)
        i0 = o_idx + IDX_HEADS * IDX_HEAD_DIM
        w_ik = wl[:, i0:i0 + IDX_HEAD_DIM]
        w_i = jnp.concatenate([wl[:, o_idx:i0], w_ik, w_ik,
                               _pad_cols(wl[:, i0 + IDX_HEAD_DIM:o_rwkv], LANES)], axis=1)
        mu = rwkv_mu[l]
        mu_p = jnp.concatenate([
            mu[:3 * D_RWKV],
            _pad_vec(mu[3 * D_RWKV:3 * D_RWKV + LORA_DECAY], LANES),
            _pad_vec(mu[3 * D_RWKV + LORA_DECAY:3 * D_RWKV + LORA_DECAY + LORA_AAA], LANES),
            mu[3 * D_RWKV + LORA_DECAY + LORA_AAA:],
            _pad_vec(vres_mu[l - 1] if l > 0 else jnp.zeros((LORA_MV,), F32), ZR_WIDTH - ZR_XV)])

        h = _rms_norm(xs, norm_mix[l], BF16)
        zb = _matmul(h, w_b, BF16, 1024, 512, "in_proj_bf16")
        zr = _matmul(h, w_r, F32, 1024, 768, "in_proj_rwkv")
        zi = _matmul(h, w_i, F32, 1024, 256, "in_proj_index")

        a_mix = _conformer_conv(zb, dw_weight[l], dw_bias[l], conv_ln_g[l], conv_ln_b[l])

        qr, kr, iqr, ik2, iws = _att_prep(zb, zi)
        b_mix = _dsa_attention(qr, iqr, iws, ik2, kr, zb)

        vres = None
        if l > 0:
            vres = (v_first, _pad_rows(vres_up[l - 1], LANES).astype(BF16), vres_bias[l - 1])
        rt, at, bt, kt, v_rwkv, bonus, gate, pc = _rwkv_prep(
            zr, mu_p, rwkv_w0[l], _pad_rows(rwkv_w2[l], LANES).astype(BF16), rwkv_a0[l],
            _pad_rows(rwkv_a2[l], LANES).astype(BF16), rwkv_g2[l].astype(BF16),
            rwkv_k_k[l], rwkv_k_a[l], rwkv_r_k[l].reshape(-1), vres)
        if l == 0:
            v_first = v_rwkv
        rh, y0, g_mat, h_mat = _scan_a(rt, at, bt, kt, v_rwkv, pc)
        c_mix = _scan_b(rh, y0, g_mat, h_mat, bonus, gate, rwkv_ln_g[l], rwkv_ln_b[l])

        merged = _merge(a_mix, b_mix, c_mix, w_conv_out[l].astype(BF16), w_att_out[l].astype(BF16),
                        w_rwkv_out[l].astype(BF16), zb, 1024, 512)
        xs = _matmul_residual(merged, w_out[l].astype(BF16), xs, 1024, 512, "out_proj")

        h2 = _rms_norm(xs, norm_ffn[l], BF16)
        act = _ffn1(h2, _pad_cols(w_ffn_gate[l].astype(BF16), D_FF_PAD),
                    _pad_cols(w_ffn_up[l].astype(BF16), D_FF_PAD), 1024, 512)
        xs = _matmul_residual(act, _pad_rows(w_ffn_down[l].astype(BF16), D_FF_PAD), xs, 512, 256,
                              "ffn_down")
    return _rms_norm(xs, norm_final, F32).reshape(B, L, D)
```

```python
import functools

import jax
import jax.numpy as jnp
import numpy as np
from jax import lax
from jax.experimental import pallas as pl
from jax.experimental.pallas import tpu as pltpu

F32 = jnp.float32
BF16 = jnp.bfloat16

D_MODEL = 4096
RMS_EPS = 1e-6
LN_EPS = 1e-5
D_CONV = D_MODEL // 4
CONV_WIDTH = 31
ATT_HEADS = 8
ATT_HEAD_DIM = 128
D_ATT = ATT_HEADS * ATT_HEAD_DIM
ATT_SCALE = ATT_HEAD_DIM ** -0.5
LOG2_E = 1.4426950408889634
IDX_HEADS = 16
IDX_HEAD_DIM = 64
IDX_SCALE = IDX_HEAD_DIM ** -0.5
IDX_W_SCALE = IDX_HEADS ** -0.5
TOPK_MAX = 256
ROPE_THETA = 500000.0
ROPE_FRACTION = 4
D_RWKV = D_MODEL // 2
RWKV_HEAD = 64
LORA_DECAY = 96
LORA_AAA = 96
LORA_MV = 64
LORA_GATE = 256
RWKV_GN_EPS = 64e-5
D_FF = ((8 * D_MODEL + 3 * 256 - 1) // (3 * 256)) * 256

LANES = 128
CHUNK = 64
IN_ATT = 2 * D_CONV
IN_IDX = IN_ATT + 3 * D_ATT
IN_RWKV = IN_IDX + IDX_HEADS * IDX_HEAD_DIM + IDX_HEAD_DIM + IDX_HEADS
IN_GATE = IN_RWKV + 3 * D_RWKV + LORA_DECAY + LORA_AAA + LORA_GATE
D_IN = IN_GATE + 3 * D_MODEL
ZR_XW = 3 * D_RWKV
ZR_XA = ZR_XW + LORA_DECAY
ZR_XG = ZR_XA + LORA_AAA
ZR_XV = ZR_XG + LORA_GATE
ZR_WIDTH = ZR_XV + LORA_MV
ZR_XA_WIN = (ZR_XW, ZR_XW + 2 * LANES)
ZR_XG_WIN = (ZR_XW + LANES, ZR_WIDTH)
ZR_XV_WIN = (ZR_WIDTH - LANES, ZR_WIDTH)
ZI_IK = IDX_HEADS * IDX_HEAD_DIM
ZI_IW_LANE = IDX_HEAD_DIM
ZI_WIDTH = ZI_IK + 2 * LANES
ZB_GATE = 2 * D_CONV + 3 * D_ATT
ZB_WIDTH = ZB_GATE + 3 * D_MODEL
VMEM_LIMIT = 56 * 1024 * 1024
INT_MIN = -2147483648
NEG_BIG = -1e30

NT_DIMS = (((1,), (1,)), ((), ()))
TN_DIMS = (((0,), (0,)), ((), ()))


def _params(sem, vmem=VMEM_LIMIT):
    return pltpu.CompilerParams(dimension_semantics=sem, vmem_limit_bytes=vmem)


def _dot(a, b):
    return jnp.dot(a, b, preferred_element_type=F32)


def _split2(x):
    hi = x.astype(BF16)
    lo = (x - hi.astype(F32)).astype(BF16)
    return hi, lo


def _dot_exact_rhs(x, e):
    h1 = x.astype(BF16)
    r1 = x - h1.astype(F32)
    h2 = r1.astype(BF16)
    h3 = (r1 - h2.astype(F32)).astype(BF16)
    return _dot(h1, e) + _dot(h2, e) + _dot(h3, e)


def _dot_exact_lhs(e, x):
    h1 = x.astype(BF16)
    r1 = x - h1.astype(F32)
    h2 = r1.astype(BF16)
    h3 = (r1 - h2.astype(F32)).astype(BF16)
    return _dot(e, h1) + _dot(e, h2) + _dot(e, h3)


def _dot3(a, b):
    ah, al = _split2(a)
    bh, bl = _split2(b)
    return _dot(ah, bh) + _dot(ah, bl) + _dot(al, bh)


def _sigmoid(x):
    return 1.0 / (1.0 + jnp.exp(-x))


def _rms_kernel(x_ref, g_ref, o_ref):
    x = x_ref[...]
    y = x * lax.rsqrt(jnp.mean(x * x, axis=-1, keepdims=True) + RMS_EPS)
    o_ref[...] = (y * g_ref[...]).astype(o_ref.dtype)


def _rms_norm(x, g, out_dtype):
    L, D = x.shape
    tr = min(256, L)
    return pl.pallas_call(
        _rms_kernel,
        grid=(L // tr,),
        in_specs=[pl.BlockSpec((tr, D), lambda i: (i, 0)),
                  pl.BlockSpec((1, D), lambda i: (0, 0))],
        out_specs=pl.BlockSpec((tr, D), lambda i: (i, 0)),
        out_shape=jax.ShapeDtypeStruct((L, D), out_dtype),
        compiler_params=_params(("parallel",)),
        name="rms_norm",
    )(x, g.reshape(1, D))


def _mm_kernel(a_ref, b_ref, o_ref):
    o_ref[...] = _dot(a_ref[...], b_ref[...]).astype(o_ref.dtype)


def _matmul(a, b, out_dtype, tm, tn, name):
    M, K = a.shape
    N = b.shape[1]
    tm = min(tm, M)
    return pl.pallas_call(
        _mm_kernel,
        grid=(M // tm, N // tn),
        in_specs=[pl.BlockSpec((tm, K), lambda i, j: (i, 0)),
                  pl.BlockSpec((K, tn), lambda i, j: (0, j))],
        out_specs=pl.BlockSpec((tm, tn), lambda i, j: (i, j)),
        out_shape=jax.ShapeDtypeStruct((M, N), out_dtype),
        compiler_params=_params(("parallel", "parallel")),
        name=name,
    )(a, b)


def _mm_res_kernel(a_ref, b_ref, x_ref, o_ref):
    o_ref[...] = x_ref[...] + _dot(a_ref[...], b_ref[...])


def _matmul_residual(a, b, x, tm, tn, name):
    M, K = a.shape
    N = b.shape[1]
    tm = min(tm, M)
    return pl.pallas_call(
        _mm_res_kernel,
        grid=(M // tm, N // tn),
        in_specs=[pl.BlockSpec((tm, K), lambda i, j: (i, 0)),
                  pl.BlockSpec((K, tn), lambda i, j: (0, j)),
                  pl.BlockSpec((tm, tn), lambda i, j: (i, j))],
        out_specs=pl.BlockSpec((tm, tn), lambda i, j: (i, j)),
        out_shape=jax.ShapeDtypeStruct((M, N), F32),
        compiler_params=_params(("parallel", "parallel")),
        name=name,
    )(a, b, x)


def _ffn1_kernel(h_ref, wg_ref, wu_ref, o_ref):
    h = h_ref[...]
    g = _dot(h, wg_ref[...])
    u = _dot(h, wu_ref[...])
    o_ref[...] = (g * _sigmoid(g) * u).astype(o_ref.dtype)


def _ffn1(h, wg, wu, tm, tn):
    M, K = h.shape
    N = wg.shape[1]
    tm = min(tm, M)
    return pl.pallas_call(
        _ffn1_kernel,
        grid=(M // tm, N // tn),
        in_specs=[pl.BlockSpec((tm, K), lambda i, j: (i, 0)),
                  pl.BlockSpec((K, tn), lambda i, j: (0, j)),
                  pl.BlockSpec((K, tn), lambda i, j: (0, j))],
        out_specs=pl.BlockSpec((tm, tn), lambda i, j: (i, j)),
        out_shape=jax.ShapeDtypeStruct((M, N), BF16),
        compiler_params=_params(("parallel", "parallel")),
        name="ffn_gate_up",
    )(h, wg, wu)


def _merge_kernel(a_ref, b_ref, c_ref, wa_ref, wb_ref, wc_ref, g0_ref, g1_ref, g2_ref, o_ref):
    ya = _dot(a_ref[...], wa_ref[...])
    yb = _dot(b_ref[...], wb_ref[...])
    yc = _dot(c_ref[...], wc_ref[...])
    m = (_sigmoid(g0_ref[...].astype(F32)) * ya + _sigmoid(g1_ref[...].astype(F32)) * yb
         + _sigmoid(g2_ref[...].astype(F32)) * yc)
    o_ref[...] = m.astype(o_ref.dtype)


def _merge(a, b, c, wa, wb, wc, zb, tm, tn):
    M = a.shape[0]
    tm = min(tm, M)
    g_off = ZB_GATE // tn
    g_stride = D_MODEL // tn
    return pl.pallas_call(
        _merge_kernel,
        grid=(M // tm, D_MODEL // tn),
        in_specs=[pl.BlockSpec((tm, D_CONV), lambda i, j: (i, 0)),
                  pl.BlockSpec((tm, D_ATT), lambda i, j: (i, 0)),
                  pl.BlockSpec((tm, D_RWKV), lambda i, j: (i, 0)),
                  pl.BlockSpec((D_CONV, tn), lambda i, j: (0, j)),
                  pl.BlockSpec((D_ATT, tn), lambda i, j: (0, j)),
                  pl.BlockSpec((D_RWKV, tn), lambda i, j: (0, j)),
                  pl.BlockSpec((tm, tn), lambda i, j: (i, g_off + j)),
                  pl.BlockSpec((tm, tn), lambda i, j: (i, g_off + g_stride + j)),
                  pl.BlockSpec((tm, tn), lambda i, j: (i, g_off + 2 * g_stride + j))],
        out_specs=pl.BlockSpec((tm, tn), lambda i, j: (i, j)),
        out_shape=jax.ShapeDtypeStruct((M, D_MODEL), BF16),
        compiler_params=_params(("parallel", "parallel")),
        name="gated_merge",
    )(a, b, c, wa, wb, wc, zb, zb, zb)


def _conv_kernel(val_ref, gate_ref, hval_ref, hgate_ref, w_ref, b_ref, lg_ref, lb_ref, o_ref,
                 c_ref, *, T):
    i = pl.program_id(0)
    halo = 32
    hv = hval_ref[...].astype(F32)
    hg = hgate_ref[...].astype(F32)
    hc = hv * _sigmoid(hg)
    c_ref[0:halo, :] = jnp.where(i > 0, hc, 0.0)
    v = val_ref[...].astype(F32)
    g = gate_ref[...].astype(F32)
    c_ref[halo:halo + T, :] = v * _sigmoid(g)
    acc = jnp.zeros((T, D_CONV), F32) + b_ref[...]
    for j in range(CONV_WIDTH):
        off = halo - (CONV_WIDTH - 1) + j
        acc = acc + c_ref[off:off + T, :] * w_ref[j:j + 1, :]
    m = jnp.mean(acc, axis=-1, keepdims=True)
    d = acc - m
    var = jnp.mean(d * d, axis=-1, keepdims=True)
    y = d * lax.rsqrt(var + LN_EPS) * lg_ref[...] + lb_ref[...]
    o_ref[...] = (y * _sigmoid(y)).astype(o_ref.dtype)


def _conformer_conv(zb, dw_w, dw_b, ln_g, ln_b):
    L = zb.shape[0]
    T = min(256, L)
    hb = T // 32
    w_pad = jnp.zeros((32, D_CONV), F32).at[:CONV_WIDTH].set(dw_w)
    return pl.pallas_call(
        functools.partial(_conv_kernel, T=T),
        grid=(L // T,),
        in_specs=[pl.BlockSpec((T, D_CONV), lambda i: (i, 0)),
                  pl.BlockSpec((T, D_CONV), lambda i: (i, 1)),
                  pl.BlockSpec((32, D_CONV), lambda i: (jnp.maximum(i * hb - 1, 0), 0)),
                  pl.BlockSpec((32, D_CONV), lambda i: (jnp.maximum(i * hb - 1, 0), 1)),
                  pl.BlockSpec((32, D_CONV), lambda i: (0, 0)),
                  pl.BlockSpec((1, D_CONV), lambda i: (0, 0)),
                  pl.BlockSpec((1, D_CONV), lambda i: (0, 0)),
                  pl.BlockSpec((1, D_CONV), lambda i: (0, 0))],
        out_specs=pl.BlockSpec((T, D_CONV), lambda i: (i, 0)),
        out_shape=jax.ShapeDtypeStruct((L, D_CONV), BF16),
        scratch_shapes=[pltpu.VMEM((T + 32, D_CONV), F32)],
        compiler_params=_params(("parallel",)),
        name="conformer_conv",
    )(zb, zb, zb, zb, w_pad, dw_b.reshape(1, -1), ln_g.reshape(1, -1), ln_b.reshape(1, -1))


def _rope_tables(pos, inv_freq, head_dim):
    rd = head_dim // ROPE_FRACTION
    half = rd // 2
    ang = pos * inv_freq
    cos = jnp.cos(ang)
    sin = jnp.sin(ang)
    lane = lax.broadcasted_iota(jnp.int32, ang.shape, 1) % head_dim
    c = jnp.where(lane < rd, cos, 1.0)
    s_up = jnp.where(lane < half, -sin, 0.0)
    s_dn = jnp.where((lane >= half) & (lane < rd), sin, 0.0)
    return c, s_up, s_dn


def _rope_apply(x, tabs, half):
    c, s_up, s_dn = tabs
    return (x * c + pltpu.roll(x, LANES - half, 1) * s_up + pltpu.roll(x, half, 1) * s_dn)


def _attprep_kernel(q_ref, k_ref, zi_ref, fq_ref, fi_ref, qo_ref, ko_ref, iqo_ref, iko_ref,
                    iwo_ref, *, T):
    i = pl.program_id(0)
    pos = (i * T + lax.broadcasted_iota(jnp.int32, (T, LANES), 0)).astype(F32)
    tq = _rope_tables(pos, fq_ref[...], ATT_HEAD_DIM)
    ti = _rope_tables(pos, fi_ref[...], IDX_HEAD_DIM)
    hq = ATT_HEAD_DIM // ROPE_FRACTION // 2
    hi = IDX_HEAD_DIM // ROPE_FRACTION // 2
    for h in range(ATT_HEADS):
        sl = slice(h * LANES, (h + 1) * LANES)
        qo_ref[:, sl] = (_rope_apply(q_ref[:, sl].astype(F32), tq, hq)
                         * (ATT_SCALE * LOG2_E)).astype(BF16)
        ko_ref[:, sl] = _rope_apply(k_ref[:, sl].astype(F32), tq, hq).astype(BF16)
    for p in range(ZI_IK // LANES):
        sl = slice(p * LANES, (p + 1) * LANES)
        iqo_ref[:, sl] = _rope_apply(zi_ref[:, sl], ti, hi).astype(BF16)
    grp = zi_ref[:, ZI_IK:ZI_IK + LANES]
    ik = _rope_apply(grp, ti, hi)
    lane = lax.broadcasted_iota(jnp.int32, (T, LANES), 1)
    iko_ref[...] = jnp.where(lane < IDX_HEAD_DIM, ik, pltpu.roll(ik, IDX_HEAD_DIM, 1)).astype(BF16)
    iwo_ref[...] = grp * (IDX_W_SCALE * IDX_SCALE)


def _lane_inv_freq(head_dim):
    rd = head_dim // ROPE_FRACTION
    half = rd // 2
    inv_freq = jnp.power(ROPE_THETA, -jnp.arange(half, dtype=F32) * (2.0 / rd))
    lane = np.arange(LANES) % head_dim
    idx = np.where(lane < rd, lane % half, 0)
    return jnp.where(jnp.asarray(lane < rd), inv_freq[idx], 0.0).reshape(1, LANES)


def _att_prep(zb, zi):
    L = zb.shape[0]
    T = min(256, L)
    row = lambda i: (i, 0)
    return pl.pallas_call(
        functools.partial(_attprep_kernel, T=T),
        grid=(L // T,),
        in_specs=[pl.BlockSpec((T, D_ATT), lambda i: (i, 2)),
                  pl.BlockSpec((T, D_ATT), lambda i: (i, 3)),
                  pl.BlockSpec((T, ZI_WIDTH), row),
                  pl.BlockSpec((1, LANES), lambda i: (0, 0)),
                  pl.BlockSpec((1, LANES), lambda i: (0, 0))],
        out_specs=[pl.BlockSpec((T, D_ATT), row), pl.BlockSpec((T, D_ATT), row),
                   pl.BlockSpec((T, ZI_IK), row), pl.BlockSpec((T, LANES), row),
                   pl.BlockSpec((T, LANES), row)],
        out_shape=[jax.ShapeDtypeStruct((L, D_ATT), BF16), jax.ShapeDtypeStruct((L, D_ATT), BF16),
                   jax.ShapeDtypeStruct((L, ZI_IK), BF16), jax.ShapeDtypeStruct((L, LANES), BF16),
                   jax.ShapeDtypeStruct((L, LANES), F32)],
        compiler_params=_params(("parallel",)),
        name="att_prep",
    )(zb, zb, zi, _lane_inv_freq(ATT_HEAD_DIM), _lane_inv_freq(IDX_HEAD_DIM))


def _dsa_kernel(q_ref, iq_ref, iw_ref, ik_ref, k_ref, v_ref, o_ref,
                key_ref, wb_ref, iqm_ref, thr_ref, m_ref, l_ref, acc_ref, *, Q, S, RB, topk):
    i = pl.program_id(0)
    j = pl.program_id(1)
    q_end = (i + 1) * Q

    @pl.when(j == 0)
    def _select():
        lane = lax.broadcasted_iota(jnp.int32, (Q, LANES), 1)
        lo = lane < IDX_HEAD_DIM
        zero = jnp.zeros((Q, LANES), BF16)
        for p in range(IDX_HEADS // 2):
            x = iq_ref[:, p * LANES:(p + 1) * LANES]
            iqm_ref[2 * p] = jnp.where(lo, x, zero)
            iqm_ref[2 * p + 1] = jnp.where(lo, zero, x)
        w = iw_ref[...]
        for h in range(IDX_HEADS):
            wb_ref[h] = jnp.broadcast_to(w[:, ZI_IW_LANE + h:ZI_IW_LANE + h + 1], (Q, LANES))
        row = i * Q + lax.broadcasted_iota(jnp.int32, (Q, S), 0)

        def score_chunk(c, carry):
            off = pl.multiple_of(c * S, S)
            ikc = ik_ref[pl.ds(off, S), :]
            sc = jnp.zeros((Q, S), F32)
            for h in range(IDX_HEADS):
                lg = lax.dot_general(iqm_ref[h], ikc, NT_DIMS, preferred_element_type=F32)
                sc = sc + jnp.maximum(lg, 0.0) * jnp.tile(wb_ref[h], (1, S // LANES))
            sc = jnp.where(sc == 0.0, 0.0, sc)
            bits = pltpu.bitcast(sc, jnp.int32)
            key = bits ^ ((bits >> 31) & 0x7FFFFFFF)
            col = off + lax.broadcasted_iota(jnp.int32, (Q, S), 1)
            key_ref[:, pl.ds(off, S)] = jnp.where(col <= row, key, INT_MIN)
            return carry

        nchunk = (q_end + S - 1) // S
        lax.fori_loop(0, nchunk, score_chunk, 0)

        def row_block(rb, carry):
            rows = pl.ds(pl.multiple_of(rb * RB, RB), RB)

            def bit_step(b, prefix):
                cand_u = prefix | jnp.left_shift(jnp.int32(1), 31 - b)
                cand = jnp.broadcast_to(cand_u ^ INT_MIN, (RB, LANES))

                def count(c, cnt):
                    off = pl.multiple_of(c * S, S)
                    for u in range(S // LANES):
                        kc = key_ref[rows, pl.ds(off + u * LANES, LANES)]
                        cnt = cnt + jnp.where(kc >= cand, 1.0, 0.0)
                    return cnt

                cnt = lax.fori_loop(0, nchunk, count, jnp.zeros((RB, LANES), F32))
                tot = jnp.sum(cnt, axis=1, keepdims=True)
                return jnp.where(tot >= float(topk), cand_u, prefix)

            prefix = lax.fori_loop(0, 32, bit_step, jnp.zeros((RB, 1), jnp.int32))
            thr = jnp.maximum(prefix ^ INT_MIN, INT_MIN + 1)
            thr_ref[rows, :] = jnp.broadcast_to(thr, (RB, LANES))
            return carry

        lax.fori_loop(0, Q // RB, row_block, 0)
        m_ref[...] = jnp.full(m_ref.shape, NEG_BIG, F32)
        l_ref[...] = jnp.zeros(l_ref.shape, F32)
        acc_ref[...] = jnp.zeros(acc_ref.shape, F32)

    @pl.when(j * S < q_end)
    def _attend():
        keys = key_ref[:, pl.ds(pl.multiple_of(j * S, S), S)]
        bias = jnp.where(keys >= jnp.tile(thr_ref[...], (1, S // LANES)), 0.0, NEG_BIG)
        for h in range(ATT_HEADS):
            sl = slice(h * LANES, (h + 1) * LANES)
            s = lax.dot_general(q_ref[:, sl], k_ref[:, sl], NT_DIMS, preferred_element_type=F32)
            s = s + bias
            m_prev = m_ref[h]
            m_new = jnp.maximum(m_prev, jnp.max(s, axis=1, keepdims=True))
            alpha = jnp.exp2(m_prev - m_new)
            p = jnp.exp2(s - jnp.tile(m_new, (1, S // LANES)))
            l_ref[h] = alpha * l_ref[h] + jnp.sum(p, axis=1, keepdims=True)
            acc_ref[:, sl] = alpha * acc_ref[:, sl] + _dot(p.astype(BF16), v_ref[:, sl])
            m_ref[h] = m_new

    @pl.when(j == pl.num_programs(1) - 1)
    def _finish():
        for h in range(ATT_HEADS):
            sl = slice(h * LANES, (h + 1) * LANES)
            o_ref[:, sl] = (acc_ref[:, sl] / l_ref[h]).astype(o_ref.dtype)


def _dsa_attention(qr, iqr, iws, ik2, kr, zb):
    L = qr.shape[0]
    Q = min(256, L)
    S = min(512, L)
    RB = 128
    topk = min(TOPK_MAX, L // 4)
    nk = L // S

    def kv_block(i, j):
        return jnp.minimum(j, ((i + 1) * Q - 1) // S)

    return pl.pallas_call(
        functools.partial(_dsa_kernel, Q=Q, S=S, RB=RB, topk=topk),
        grid=(L // Q, nk),
        in_specs=[pl.BlockSpec((Q, D_ATT), lambda i, j: (i, 0)),
                  pl.BlockSpec((Q, ZI_IK), lambda i, j: (i, 0)),
                  pl.BlockSpec((Q, LANES), lambda i, j: (i, 0)),
                  pl.BlockSpec((L, LANES), lambda i, j: (0, 0)),
                  pl.BlockSpec((S, D_ATT), lambda i, j: (kv_block(i, j), 0)),
                  pl.BlockSpec((S, D_ATT), lambda i, j: (kv_block(i, j), 4))],
        out_specs=pl.BlockSpec((Q, D_ATT), lambda i, j: (i, 0)),
        out_shape=jax.ShapeDtypeStruct((L, D_ATT), BF16),
        scratch_shapes=[pltpu.VMEM((Q, L), jnp.int32),
                        pltpu.VMEM((IDX_HEADS, Q, LANES), F32),
                        pltpu.VMEM((IDX_HEADS, Q, LANES), BF16),
                        pltpu.VMEM((Q, LANES), jnp.int32),
                        pltpu.VMEM((ATT_HEADS, Q, LANES), F32),
                        pltpu.VMEM((ATT_HEADS, Q, LANES), F32),
                        pltpu.VMEM((Q, D_ATT), F32)],
        compiler_params=_params(("parallel", "arbitrary")),
        name="dsa_attention",
    )(qr, iqr, iws, ik2, kr, zb)


def _head_sum(x, e):
    cols = []
    for p in range(x.shape[1] // LANES):
        cols.append(_dot_exact_rhs(x[:, p * LANES:(p + 1) * LANES], e))
    return jnp.concatenate(cols, axis=1)


def _rwkv_prep_kernel(*refs, T, has_vres):
    if has_vres:
        (z_ref, halo_ref, mu_ref, w0_ref, w2_ref, a0_ref, a2_ref, g2_ref, kk_ref, ka_ref, rk_ref,
         vf_ref, vup_ref, vb_ref,
         rt_ref, at_ref, bt_ref, kt_ref, v_ref, bonus_ref, g_ref, pc_ref) = refs
    else:
        (z_ref, halo_ref, mu_ref, w0_ref, w2_ref, a0_ref, a2_ref, g2_ref, kk_ref, ka_ref, rk_ref,
         rt_ref, at_ref, bt_ref, kt_ref, v_ref, bonus_ref, g_ref, pc_ref) = refs
    i = pl.program_id(0)
    D = D_RWKV

    def shifted(lo, hi):
        z = z_ref[:, lo:hi]
        first = jnp.where(i > 0, halo_ref[7:8, lo:hi], 0.0)
        rows = lax.broadcasted_iota(jnp.int32, z.shape, 0)
        prev = jnp.where(rows == 0, first, pltpu.roll(z, 1, 0))
        return z + (prev - z) * mu_ref[:, lo:hi]

    r = shifted(0, D)
    kraw = shifted(D, 2 * D)
    v = shifted(2 * D, 3 * D)
    xw = shifted(ZR_XW, ZR_XW + LANES)
    xa = shifted(*ZR_XA_WIN)
    xg = shifted(*ZR_XG_WIN)
    if has_vres:
        xv = shifted(*ZR_XV_WIN)
        mix = _sigmoid(vb_ref[...] + _dot(xv.astype(BF16), vup_ref[...]))
        v = v + (vf_ref[...] - v) * mix
    v_ref[...] = v

    w_in = w0_ref[...] + _dot(jnp.tanh(xw).astype(BF16), w2_ref[...])
    w_log = -(jnp.maximum(-w_in, 0.0) + jnp.log(1.0 + jnp.exp(-jnp.abs(w_in)))) - 0.5
    logw = -jnp.exp(w_log)
    a_lr = _sigmoid(a0_ref[...] + _dot(xa.astype(BF16), a2_ref[...]))
    g_ref[...] = _dot(_sigmoid(xg).astype(BF16), g2_ref[...])

    li = lax.broadcasted_iota(jnp.int32, (LANES, LANES), 0) // RWKV_HEAD
    lj = lax.broadcasted_iota(jnp.int32, (LANES, LANES), 1) // RWKV_HEAD
    e_head = jnp.where(li == lj, 1.0, 0.0).astype(BF16)
    kk = kraw * kk_ref[...]
    norm = jnp.sqrt(_head_sum(kk * kk, e_head))
    kk = kk / jnp.maximum(norm, 1e-12)
    k = kraw * (1.0 + (a_lr - 1.0) * ka_ref[...])
    bonus_ref[...] = _head_sum(r * k * rk_ref[...], e_head) * v

    ti = lax.broadcasted_iota(jnp.int32, (T, T), 0)
    tj = lax.broadcasted_iota(jnp.int32, (T, T), 1)
    tri = jnp.where((ti // CHUNK == tj // CHUNK) & (tj <= ti), 1.0, 0.0).astype(BF16)
    cum = _dot_exact_lhs(tri, logw)
    p_in = jnp.exp(cum)
    p_out = jnp.exp(-cum)
    rt_ref[...] = r * p_in
    at_ref[...] = -kk * jnp.exp(cum - logw)
    bt_ref[...] = kk * a_lr * p_out
    kt_ref[...] = k * p_out
    for c in range(T // CHUNK):
        pc_ref[c] = p_in[(c + 1) * CHUNK - 1:(c + 1) * CHUNK, :]


def _rwkv_prep(zr, mu, w0, w2, a0, a2, g2, k_k, k_a, r_k, vres):
    L = zr.shape[0]
    T = min(128, L)
    D = D_RWKV
    row = lambda i: (i, 0)
    const = lambda i: (0, 0)
    vec = lambda a: a.reshape(1, -1)
    has_vres = vres is not None
    ins = [zr, zr, vec(mu), vec(w0), w2, vec(a0), a2, g2, vec(k_k), vec(k_a), vec(r_k)]
    in_specs = [pl.BlockSpec((T, ZR_WIDTH), row),
                pl.BlockSpec((8, ZR_WIDTH), lambda i: (jnp.maximum(i * (T // 8) - 1, 0), 0)),
                pl.BlockSpec((1, ZR_WIDTH), const),
                pl.BlockSpec((1, D), const), pl.BlockSpec(w2.shape, const),
                pl.BlockSpec((1, D), const), pl.BlockSpec(a2.shape, const),
                pl.BlockSpec(g2.shape, const),
                pl.BlockSpec((1, D), const), pl.BlockSpec((1, D), const), pl.BlockSpec((1, D), const)]
    if has_vres:
        v_first, v_up, v_bias = vres
        ins += [v_first, v_up, vec(v_bias)]
        in_specs += [pl.BlockSpec((T, D), row), pl.BlockSpec((LANES, D), const),
                     pl.BlockSpec((1, D), const)]
    big = jax.ShapeDtypeStruct((L, D), F32)
    return pl.pallas_call(
        functools.partial(_rwkv_prep_kernel, T=T, has_vres=has_vres),
        grid=(L // T,),
        in_specs=in_specs,
        out_specs=[pl.BlockSpec((T, D), row)] * 7
                  + [pl.BlockSpec((T // CHUNK, 1, D), lambda i: (i, 0, 0))],
        out_shape=[big] * 7 + [jax.ShapeDtypeStruct((L // CHUNK, 1, D), F32)],
        compiler_params=_params(("parallel",)),
        name="rwkv_prep",
    )(*ins)


def _stack2(x, lo):
    zero = jnp.zeros_like(x)
    return jnp.concatenate([jnp.where(lo, x, zero), jnp.where(lo, zero, x)], axis=0)


def _scan_a_kernel(rt_ref, at_ref, bt_ref, kt_ref, v_ref, pc_ref, rh_ref, y0_ref, g_ref, h_ref,
                   *, CB, UNROLL):
    C = CHUNK
    lane = lax.broadcasted_iota(jnp.int32, (C, LANES), 1)
    lo = lane < RWKV_HEAD
    spos = lane % RWKV_HEAD
    tpos = lax.broadcasted_iota(jnp.int32, (C, LANES), 0)
    strict = spos < tpos
    incl = spos <= tpos
    eye_c = jnp.where(spos == tpos, 1.0, 0.0)
    ri = lax.broadcasted_iota(jnp.int32, (LANES, LANES), 0)
    ci = lax.broadcasted_iota(jnp.int32, (LANES, LANES), 1)
    same_head = (ri // RWKV_HEAD) == (ci // RWKV_HEAD)
    diag = ri == ci

    def stk(x):
        return _stack2(x, lo).astype(BF16)

    nt = lambda a, b: lax.dot_general(a, b, NT_DIMS, preferred_element_type=F32)
    tn = lambda a, b: lax.dot_general(a, b, TN_DIMS, preferred_element_type=F32)

    def each(fn, *cols):
        return [fn(*args) for args in zip(*cols)]

    def body(cg, carry):
        cs = [cg * UNROLL + u for u in range(UNROLL)]
        rows = [pl.ds(pl.multiple_of(c * C, C), C) for c in cs]
        rt = [rt_ref[r, :] for r in rows]
        at = [at_ref[r, :] for r in rows]
        bt = [bt_ref[r, :] for r in rows]
        kt = [kt_ref[r, :] for r in rows]
        v = [v_ref[r, :] for r in rows]
        pc = [pc_ref[c] for c in cs]
        at_b = each(lambda x: x.astype(BF16), at)
        rt_b = each(lambda x: x.astype(BF16), rt)
        bs = each(stk, bt)
        ks = each(stk, kt)
        vs = each(stk, v)
        m_ab = each(lambda a, b: jnp.where(strict, nt(a, b), 0.0), at_b, bs)
        m_ak = each(lambda a, b: jnp.where(strict, nt(a, b), 0.0), at_b, ks)
        m_rb = each(lambda a, b: jnp.where(incl, nt(a, b), 0.0), rt_b, bs)
        m_rk = each(lambda a, b: jnp.where(incl, nt(a, b), 0.0), rt_b, ks)
        mv = each(lambda m, x: _dot(m.astype(BF16), x), m_ak, vs)
        pw = m_ab
        inv = each(lambda m: eye_c + m, pw)
        for _ in range(5):
            pw = each(lambda x: _dot(x.astype(BF16), stk(x)), pw)
            inv = each(lambda t, x: t + _dot(t.astype(BF16), stk(x)), inv, pw)
        inv_b = each(lambda x: x.astype(BF16), inv)
        w = each(lambda t, x: _dot(t, stk(x)), inv_b, at)
        u0 = each(lambda t, x: _dot(t, stk(x)), inv_b, mv)
        m_rb_b = each(lambda x: x.astype(BF16), m_rb)
        rh = each(lambda r, m, x: r + _dot(m, stk(x)), rt, m_rb_b, w)
        y0 = each(lambda m, x, m2, x2: _dot(m, stk(x)) + _dot(m2.astype(BF16), x2),
                  m_rb_b, u0, m_rk, vs)
        btp = each(lambda x, p: (x * p).astype(BF16), bt, pc)
        ktp = each(lambda x, p: (x * p).astype(BF16), kt, pc)
        gm = each(lambda b, x, p: jnp.where(diag, jnp.broadcast_to(p, (LANES, LANES)), 0.0)
                  + jnp.where(same_head, tn(b, x.astype(BF16)), 0.0), btp, w, pc)
        hm = each(lambda b, x, k, y: jnp.where(same_head, tn(b, x.astype(BF16))
                                               + tn(k, y.astype(BF16)), 0.0), btp, u0, ktp, v)
        for u in range(UNROLL):
            rh_ref[rows[u], :] = rh[u]
            y0_ref[rows[u], :] = y0[u]
            g_ref[cs[u], 0] = gm[u]
            h_ref[cs[u], 0] = hm[u]
        return carry

    lax.fori_loop(0, CB // UNROLL, body, 0)


def _scan_a(rt, at, bt, kt, v, pc):
    L, D = rt.shape
    NP = D // LANES
    NC = L // CHUNK
    CB = min(8, NC)
    blk = pl.BlockSpec((CB * CHUNK, LANES), lambda ci, p: (ci, p))
    gh = pl.BlockSpec((CB, 1, LANES, LANES), lambda ci, p: (ci, p, 0, 0))
    return pl.pallas_call(
        functools.partial(_scan_a_kernel, CB=CB, UNROLL=min(8, CB)),
        grid=(NC // CB, NP),
        in_specs=[blk] * 5 + [pl.BlockSpec((CB, 1, LANES), lambda ci, p: (ci, 0, p))],
        out_specs=[blk, blk, gh, gh],
        out_shape=[jax.ShapeDtypeStruct((L, D), F32), jax.ShapeDtypeStruct((L, D), F32),
                   jax.ShapeDtypeStruct((NC, NP, LANES, LANES), F32),
                   jax.ShapeDtypeStruct((NC, NP, LANES, LANES), F32)],
        compiler_params=_params(("parallel", "parallel")),
        name="rwkv_chunk_local",
    )(rt, at, bt, kt, v, pc)


def _scan_b_kernel(rh_ref, y0_ref, g_ref, h_ref, bonus_ref, gate_ref, lg_ref, lb_ref, o_ref,
                   st_ref, *, CB, PP):
    ci = pl.program_id(1)

    @pl.when(ci == 0)
    def _():
        st_ref[...] = jnp.zeros(st_ref.shape, F32)

    li = lax.broadcasted_iota(jnp.int32, (LANES, LANES), 0) // RWKV_HEAD
    lj = lax.broadcasted_iota(jnp.int32, (LANES, LANES), 1) // RWKV_HEAD
    e_mean = jnp.where(li == lj, 1.0 / RWKV_HEAD, 0.0).astype(BF16)

    def body(c, carry):
        rows = pl.ds(pl.multiple_of(c * CHUNK, CHUNK), CHUNK)
        sls = [slice(p * LANES, (p + 1) * LANES) for p in range(PP)]

        def each(fn, *cols):
            return [fn(*args) for args in zip(*cols)]

        st = [st_ref[p] for p in range(PP)]
        y = each(lambda sl, s: _dot3(rh_ref[rows, sl], s) + y0_ref[rows, sl], sls, st)
        st_new = each(lambda p, s: _dot3(g_ref[c, p], s) + h_ref[c, p], list(range(PP)), st)
        for p in range(PP):
            st_ref[p] = st_new[p]
        mean = each(lambda x: _dot_exact_rhs(x, e_mean), y)
        d = each(lambda x, m: x - m, y, mean)
        var = each(lambda x: _dot_exact_rhs(x * x, e_mean), d)
        for p in range(PP):
            sl = sls[p]
            yn = d[p] * lax.rsqrt(var[p] + RWKV_GN_EPS) * lg_ref[:, sl] + lb_ref[:, sl]
            o_ref[rows, sl] = ((yn + bonus_ref[rows, sl]) * gate_ref[rows, sl]).astype(o_ref.dtype)
        return carry

    lax.fori_loop(0, CB, body, 0)


def _scan_b(rh, y0, g, h, bonus, gate, ln_g, ln_b):
    L, D = rh.shape
    NP = D // LANES
    NC = L // CHUNK
    CB = min(8, NC)
    PP = 8
    blk = pl.BlockSpec((CB * CHUNK, PP * LANES), lambda pg, ci: (ci, pg))
    gh = pl.BlockSpec((CB, PP, LANES, LANES), lambda pg, ci: (ci, pg, 0, 0))
    vec = pl.BlockSpec((1, PP * LANES), lambda pg, ci: (0, pg))
    return pl.pallas_call(
        functools.partial(_scan_b_kernel, CB=CB, PP=PP),
        grid=(NP // PP, NC // CB),
        in_specs=[blk, blk, gh, gh, blk, blk, vec, vec],
        out_specs=blk,
        out_shape=jax.ShapeDtypeStruct((L, D), BF16),
        scratch_shapes=[pltpu.VMEM((PP, LANES, LANES), F32)],
        compiler_params=_params(("parallel", "arbitrary")),
        name="rwkv_state_scan",
    )(rh, y0, g, h, bonus, gate, ln_g.reshape(1, -1), ln_b.reshape(1, -1))


def _pack_kernel(w_ref, vd_ref, tail_ref, wb_ref, wr_ref, wi_ref):
    tr = wb_ref.shape[0]
    wb_ref[:, 0:IN_IDX] = w_ref[:, 0:IN_IDX].astype(BF16)
    wi_ref[...] = w_ref[:, IN_IDX:IN_IDX + ZI_WIDTH].astype(BF16)
    g0 = (IN_GATE // LANES) * LANES
    gwin = 3 * D_MODEL
    rolled = pltpu.roll(w_ref[:, g0:g0 + gwin], gwin - (IN_GATE - g0), 1)
    wb_ref[:, IN_IDX:IN_IDX + gwin - LANES] = rolled[:, 0:gwin - LANES].astype(BF16)
    wb_ref[:, IN_IDX + gwin - LANES:] = tail_ref[...].astype(BF16)
    r0 = (IN_RWKV // LANES) * LANES
    rwin = ZR_WIDTH + LANES
    rolled = pltpu.roll(w_ref[:, r0:r0 + rwin], rwin - (IN_RWKV - r0), 1)
    wr_ref[:, 0:ZR_WIDTH - LANES] = rolled[:, 0:ZR_WIDTH - LANES].astype(BF16)
    lane = lax.broadcasted_iota(jnp.int32, (tr, LANES), 1)
    last = jnp.where(lane < LANES - LORA_MV, rolled[:, ZR_WIDTH - LANES:ZR_WIDTH], vd_ref[...])
    wr_ref[:, ZR_WIDTH - LANES:] = last.astype(BF16)


def _pack_in_proj(w_in, l, vdown):
    K = w_in.shape[1]
    tr = 64
    vd = jnp.pad(vdown, ((0, 0), (LANES - LORA_MV, 0)))
    tail = w_in[l, :, D_IN - LANES:]
    row = lambda i: (i, 0)
    return pl.pallas_call(
        _pack_kernel,
        grid=(K // tr,),
        in_specs=[pl.BlockSpec((None, tr, D_IN), lambda i: (l, i, 0)),
                  pl.BlockSpec((tr, LANES), row), pl.BlockSpec((tr, LANES), row)],
        out_specs=[pl.BlockSpec((tr, ZB_WIDTH), row), pl.BlockSpec((tr, ZR_WIDTH), row),
                   pl.BlockSpec((tr, ZI_WIDTH), row)],
        out_shape=[jax.ShapeDtypeStruct((K, ZB_WIDTH), BF16),
                   jax.ShapeDtypeStruct((K, ZR_WIDTH), BF16),
                   jax.ShapeDtypeStruct((K, ZI_WIDTH), BF16)],
        compiler_params=_params(("parallel",)),
        name="pack_in_proj",
    )(w_in, vd, tail)


def _rows_at(w, start, height):
    return jnp.pad(w, ((start, height - start - w.shape[0]), (0, 0)))


def kernel(x, w_in, norm_mix, dw_weight, dw_bias, conv_ln_g, conv_ln_b, w_conv_out, w_att_out, rwkv_mu, rwkv_w0, rwkv_w2, rwkv_a0, rwkv_a2, rwkv_g2, rwkv_k_k, rwkv_k_a, rwkv_r_k, rwkv_ln_g, rwkv_ln_b, vres_down, vres_mu, vres_up, vres_bias, w_rwkv_out, w_out, norm_ffn, w_ffn_gate, w_ffn_up, w_ffn_down, norm_final):
    B, L, D = x.shape
    assert B == 1 and D == D_MODEL and L % 256 == 0
    depth = w_in.shape[0]
    xs = x.reshape(L, D)
    assert w_in.shape[1:] == (D_MODEL, D_IN)
    v_first = None
    for l in range(depth):
        vdown = vres_down[l - 1] if l > 0 else jnp.zeros((D, LORA_MV), F32)
        w_b, w_r, w_i = _pack_in_proj(w_in, l, vdown)
        mu_p = jnp.concatenate([rwkv_mu[l], vres_mu[l - 1] if l > 0 else jnp.zeros((LORA_MV,), F32)])
        w2_p = _rows_at(rwkv_w2[l], 0, LANES).astype(BF16)
        a2_p = _rows_at(rwkv_a2[l], ZR_XA - ZR_XA_WIN[0], ZR_XA_WIN[1] - ZR_XA_WIN[0]).astype(BF16)
        g2_p = _rows_at(rwkv_g2[l], ZR_XG - ZR_XG_WIN[0], ZR_XG_WIN[1] - ZR_XG_WIN[0]).astype(BF16)

        h = _rms_norm(xs, norm_mix[l], BF16)
        zb = _matmul(h, w_b, BF16, 1024, 512, "in_proj_bf16")
        zr = _matmul(h, w_r, F32, 1024, 512, "in_proj_rwkv")
        zi = _matmul(h, w_i, F32, 1024, 256, "in_proj_index")

        a_mix = _conformer_conv(zb, dw_weight[l], dw_bias[l], conv_ln_g[l], conv_ln_b[l])

        qr, kr, iqr, ik2, iws = _att_prep(zb, zi)
        b_mix = _dsa_attention(qr, iqr, iws, ik2, kr, zb)

        vres = None
        if l > 0:
            vup_p = _rows_at(vres_up[l - 1], ZR_XV - ZR_XV_WIN[0], LANES).astype(BF16)
            vres = (v_first, vup_p, vres_bias[l - 1])
        rt, at, bt, kt, v_rwkv, bonus, gate, pc = _rwkv_prep(
            zr, mu_p, rwkv_w0[l], w2_p, rwkv_a0[l], a2_p, g2_p,
            rwkv_k_k[l], rwkv_k_a[l], rwkv_r_k[l].reshape(-1), vres)
        if l == 0:
            v_first = v_rwkv
        rh, y0, g_mat, h_mat = _scan_a(rt, at, bt, kt, v_rwkv, pc)
        c_mix = _scan_b(rh, y0, g_mat, h_mat, bonus, gate, rwkv_ln_g[l], rwkv_ln_b[l])

        merged = _merge(a_mix, b_mix, c_mix, w_conv_out[l].astype(BF16), w_att_out[l].astype(BF16),
                        w_rwkv_out[l].astype(BF16), zb, 1024, 512)
        xs = _matmul_residual(merged, w_out[l].astype(BF16), xs, 1024, 512, "out_proj")

        h2 = _rms_norm(xs, norm_ffn[l], BF16)
        act = _ffn1(h2, w_ffn_gate[l].astype(BF16), w_ffn_up[l].astype(BF16), 1024, 256)
        xs = _matmul_residual(act, w_ffn_down[l].astype(BF16), xs, 512, 256, "ffn_down")
    return _rms_norm(xs, norm_final, F32).reshape(B, L, D)
```

```python
import functools

import jax
import jax.numpy as jnp
import numpy as np
from jax import lax
from jax.experimental import pallas as pl
from jax.experimental.pallas import tpu as pltpu

F32 = jnp.float32
BF16 = jnp.bfloat16

D_MODEL = 4096
RMS_EPS = 1e-6
LN_EPS = 1e-5
D_CONV = D_MODEL // 4
CONV_WIDTH = 31
ATT_HEADS = 8
ATT_HEAD_DIM = 128
D_ATT = ATT_HEADS * ATT_HEAD_DIM
ATT_SCALE = ATT_HEAD_DIM ** -0.5
LOG2_E = 1.4426950408889634
IDX_HEADS = 16
IDX_HEAD_DIM = 64
IDX_SCALE = IDX_HEAD_DIM ** -0.5
IDX_W_SCALE = IDX_HEADS ** -0.5
TOPK_MAX = 256
ROPE_THETA = 500000.0
ROPE_FRACTION = 4
D_RWKV = D_MODEL // 2
RWKV_HEAD = 64
LORA_DECAY = 96
LORA_AAA = 96
LORA_MV = 64
LORA_GATE = 256
RWKV_GN_EPS = 64e-5
D_FF = ((8 * D_MODEL + 3 * 256 - 1) // (3 * 256)) * 256

LANES = 128
CHUNK = 64
IN_ATT = 2 * D_CONV
IN_IDX = IN_ATT + 3 * D_ATT
IN_RWKV = IN_IDX + IDX_HEADS * IDX_HEAD_DIM + IDX_HEAD_DIM + IDX_HEADS
IN_GATE = IN_RWKV + 3 * D_RWKV + LORA_DECAY + LORA_AAA + LORA_GATE
D_IN = IN_GATE + 3 * D_MODEL
ZR_XW = 3 * D_RWKV
ZR_XA = ZR_XW + LORA_DECAY
ZR_XG = ZR_XA + LORA_AAA
ZR_XV = ZR_XG + LORA_GATE
ZR_WIDTH = ZR_XV + LORA_MV
ZR_XA_WIN = (ZR_XW, ZR_XW + 2 * LANES)
ZR_XG_WIN = (ZR_XW + LANES, ZR_WIDTH)
ZR_XV_WIN = (ZR_WIDTH - LANES, ZR_WIDTH)
ZI_IK = IDX_HEADS * IDX_HEAD_DIM
ZI_IW_LANE = IDX_HEAD_DIM
ZI_WIDTH = ZI_IK + 2 * LANES
ZB_GATE = 2 * D_CONV + 3 * D_ATT
ZB_WIDTH = ZB_GATE + 3 * D_MODEL
VMEM_LIMIT = 56 * 1024 * 1024
INT_MIN = -2147483648
NEG_BIG = -1e30

NT_DIMS = (((1,), (1,)), ((), ()))
TN_DIMS = (((0,), (0,)), ((), ()))


def _params(sem, vmem=VMEM_LIMIT):
    return pltpu.CompilerParams(dimension_semantics=sem, vmem_limit_bytes=vmem)


def _dot(a, b):
    return jnp.dot(a, b, preferred_element_type=F32)


def _split2(x):
    hi = x.astype(BF16)
    lo = (x - hi.astype(F32)).astype(BF16)
    return hi, lo


def _dot_exact_rhs(x, e):
    h1 = x.astype(BF16)
    r1 = x - h1.astype(F32)
    h2 = r1.astype(BF16)
    h3 = (r1 - h2.astype(F32)).astype(BF16)
    return _dot(h1, e) + _dot(h2, e) + _dot(h3, e)


def _dot_exact_lhs(e, x):
    h1 = x.astype(BF16)
    r1 = x - h1.astype(F32)
    h2 = r1.astype(BF16)
    h3 = (r1 - h2.astype(F32)).astype(BF16)
    return _dot(e, h1) + _dot(e, h2) + _dot(e, h3)


def _dot3(a, b):
    ah, al = _split2(a)
    bh, bl = _split2(b)
    return _dot(ah, bh) + _dot(ah, bl) + _dot(al, bh)


def _sigmoid(x):
    return 1.0 / (1.0 + jnp.exp(-x))


def _rms_kernel(x_ref, g_ref, o_ref):
    x = x_ref[...]
    y = x * lax.rsqrt(jnp.mean(x * x, axis=-1, keepdims=True) + RMS_EPS)
    o_ref[...] = (y * g_ref[...]).astype(o_ref.dtype)


def _rms_norm(x, g, out_dtype):
    L, D = x.shape
    tr = min(256, L)
    return pl.pallas_call(
        _rms_kernel,
        grid=(L // tr,),
        in_specs=[pl.BlockSpec((tr, D), lambda i: (i, 0)),
                  pl.BlockSpec((1, D), lambda i: (0, 0))],
        out_specs=pl.BlockSpec((tr, D), lambda i: (i, 0)),
        out_shape=jax.ShapeDtypeStruct((L, D), out_dtype),
        compiler_params=_params(("parallel",)),
        name="rms_norm",
    )(x, g.reshape(1, D))


def _mm_kernel(a_ref, b_ref, o_ref):
    o_ref[...] = _dot(a_ref[...], b_ref[...]).astype(o_ref.dtype)


def _matmul(a, b, out_dtype, tm, tn, name):
    M, K = a.shape
    N = b.shape[1]
    tm = min(tm, M)
    return pl.pallas_call(
        _mm_kernel,
        grid=(M // tm, N // tn),
        in_specs=[pl.BlockSpec((tm, K), lambda i, j: (i, 0)),
                  pl.BlockSpec((K, tn), lambda i, j: (0, j))],
        out_specs=pl.BlockSpec((tm, tn), lambda i, j: (i, j)),
        out_shape=jax.ShapeDtypeStruct((M, N), out_dtype),
        compiler_params=_params(("parallel", "parallel")),
        name=name,
    )(a, b)


def _mm_res_kernel(a_ref, b_ref, x_ref, o_ref):
    o_ref[...] = x_ref[...] + _dot(a_ref[...], b_ref[...])


def _layer_weight_spec(w, l, tn):
    return pl.BlockSpec((None, w.shape[1], tn), lambda i, j: (l, 0, j))


def _matmul_residual(a, b, l, x, tm, tn, name):
    M, K = a.shape
    N = b.shape[2]
    tm = min(tm, M)
    return pl.pallas_call(
        _mm_res_kernel,
        grid=(M // tm, N // tn),
        in_specs=[pl.BlockSpec((tm, K), lambda i, j: (i, 0)),
                  _layer_weight_spec(b, l, tn),
                  pl.BlockSpec((tm, tn), lambda i, j: (i, j))],
        out_specs=pl.BlockSpec((tm, tn), lambda i, j: (i, j)),
        out_shape=jax.ShapeDtypeStruct((M, N), F32),
        compiler_params=_params(("parallel", "parallel")),
        name=name,
    )(a, b, x)


def _ffn1_kernel(h_ref, wg_ref, wu_ref, o_ref):
    h = h_ref[...]
    g = _dot(h, wg_ref[...])
    u = _dot(h, wu_ref[...])
    o_ref[...] = (g * _sigmoid(g) * u).astype(o_ref.dtype)


def _ffn1(h, wg, wu, l, tm, tn):
    M, K = h.shape
    N = wg.shape[2]
    tm = min(tm, M)
    return pl.pallas_call(
        _ffn1_kernel,
        grid=(M // tm, N // tn),
        in_specs=[pl.BlockSpec((tm, K), lambda i, j: (i, 0)),
                  _layer_weight_spec(wg, l, tn),
                  _layer_weight_spec(wu, l, tn)],
        out_specs=pl.BlockSpec((tm, tn), lambda i, j: (i, j)),
        out_shape=jax.ShapeDtypeStruct((M, N), BF16),
        compiler_params=_params(("parallel", "parallel")),
        name="ffn_gate_up",
    )(h, wg, wu)


def _merge_kernel(a_ref, b_ref, c_ref, wa_ref, wb_ref, wc_ref, g0_ref, g1_ref, g2_ref, o_ref):
    ya = _dot(a_ref[...], wa_ref[...])
    yb = _dot(b_ref[...], wb_ref[...])
    yc = _dot(c_ref[...], wc_ref[...])
    m = (_sigmoid(g0_ref[...].astype(F32)) * ya + _sigmoid(g1_ref[...].astype(F32)) * yb
         + _sigmoid(g2_ref[...].astype(F32)) * yc)
    o_ref[...] = m.astype(o_ref.dtype)


def _merge(a, b, c, wa, wb, wc, l, zb, tm, tn):
    M = a.shape[0]
    tm = min(tm, M)
    g_off = ZB_GATE // tn
    g_stride = D_MODEL // tn
    return pl.pallas_call(
        _merge_kernel,
        grid=(M // tm, D_MODEL // tn),
        in_specs=[pl.BlockSpec((tm, D_CONV), lambda i, j: (i, 0)),
                  pl.BlockSpec((tm, D_ATT), lambda i, j: (i, 0)),
                  pl.BlockSpec((tm, D_RWKV), lambda i, j: (i, 0)),
                  _layer_weight_spec(wa, l, tn),
                  _layer_weight_spec(wb, l, tn),
                  _layer_weight_spec(wc, l, tn),
                  pl.BlockSpec((tm, tn), lambda i, j: (i, g_off + j)),
                  pl.BlockSpec((tm, tn), lambda i, j: (i, g_off + g_stride + j)),
                  pl.BlockSpec((tm, tn), lambda i, j: (i, g_off + 2 * g_stride + j))],
        out_specs=pl.BlockSpec((tm, tn), lambda i, j: (i, j)),
        out_shape=jax.ShapeDtypeStruct((M, D_MODEL), BF16),
        compiler_params=_params(("parallel", "parallel")),
        name="gated_merge",
    )(a, b, c, wa, wb, wc, zb, zb, zb)


def _conv_kernel(val_ref, gate_ref, hval_ref, hgate_ref, w_ref, b_ref, lg_ref, lb_ref, o_ref,
                 c_ref, *, T):
    i = pl.program_id(0)
    halo = 32
    hv = hval_ref[...].astype(F32)
    hg = hgate_ref[...].astype(F32)
    hc = hv * _sigmoid(hg)
    c_ref[0:halo, :] = jnp.where(i > 0, hc, 0.0)
    v = val_ref[...].astype(F32)
    g = gate_ref[...].astype(F32)
    c_ref[halo:halo + T, :] = v * _sigmoid(g)
    acc = jnp.zeros((T, D_CONV), F32) + b_ref[...]
    for j in range(CONV_WIDTH):
        off = halo - (CONV_WIDTH - 1) + j
        acc = acc + c_ref[off:off + T, :] * w_ref[j:j + 1, :]
    m = jnp.mean(acc, axis=-1, keepdims=True)
    d = acc - m
    var = jnp.mean(d * d, axis=-1, keepdims=True)
    y = d * lax.rsqrt(var + LN_EPS) * lg_ref[...] + lb_ref[...]
    o_ref[...] = (y * _sigmoid(y)).astype(o_ref.dtype)


def _conformer_conv(zb, dw_w, dw_b, ln_g, ln_b):
    L = zb.shape[0]
    T = min(256, L)
    hb = T // 32
    w_pad = jnp.zeros((32, D_CONV), F32).at[:CONV_WIDTH].set(dw_w)
    return pl.pallas_call(
        functools.partial(_conv_kernel, T=T),
        grid=(L // T,),
        in_specs=[pl.BlockSpec((T, D_CONV), lambda i: (i, 0)),
                  pl.BlockSpec((T, D_CONV), lambda i: (i, 1)),
                  pl.BlockSpec((32, D_CONV), lambda i: (jnp.maximum(i * hb - 1, 0), 0)),
                  pl.BlockSpec((32, D_CONV), lambda i: (jnp.maximum(i * hb - 1, 0), 1)),
                  pl.BlockSpec((32, D_CONV), lambda i: (0, 0)),
                  pl.BlockSpec((1, D_CONV), lambda i: (0, 0)),
                  pl.BlockSpec((1, D_CONV), lambda i: (0, 0)),
                  pl.BlockSpec((1, D_CONV), lambda i: (0, 0))],
        out_specs=pl.BlockSpec((T, D_CONV), lambda i: (i, 0)),
        out_shape=jax.ShapeDtypeStruct((L, D_CONV), BF16),
        scratch_shapes=[pltpu.VMEM((T + 32, D_CONV), F32)],
        compiler_params=_params(("parallel",)),
        name="conformer_conv",
    )(zb, zb, zb, zb, w_pad, dw_b.reshape(1, -1), ln_g.reshape(1, -1), ln_b.reshape(1, -1))


def _rope_tables(pos, inv_freq, head_dim):
    rd = head_dim // ROPE_FRACTION
    half = rd // 2
    ang = pos * inv_freq
    cos = jnp.cos(ang)
    sin = jnp.sin(ang)
    lane = lax.broadcasted_iota(jnp.int32, ang.shape, 1) % head_dim
    c = jnp.where(lane < rd, cos, 1.0)
    s_up = jnp.where(lane < half, -sin, 0.0)
    s_dn = jnp.where((lane >= half) & (lane < rd), sin, 0.0)
    return c, s_up, s_dn


def _rope_apply(x, tabs, half):
    c, s_up, s_dn = tabs
    return (x * c + pltpu.roll(x, LANES - half, 1) * s_up + pltpu.roll(x, half, 1) * s_dn)


def _attprep_kernel(q_ref, k_ref, zi_ref, fq_ref, fi_ref, qo_ref, ko_ref, iqo_ref, iko_ref,
                    iwo_ref, *, T):
    i = pl.program_id(0)
    pos = (i * T + lax.broadcasted_iota(jnp.int32, (T, LANES), 0)).astype(F32)
    tq = _rope_tables(pos, fq_ref[...], ATT_HEAD_DIM)
    ti = _rope_tables(pos, fi_ref[...], IDX_HEAD_DIM)
    hq = ATT_HEAD_DIM // ROPE_FRACTION // 2
    hi = IDX_HEAD_DIM // ROPE_FRACTION // 2
    for h in range(ATT_HEADS):
        sl = slice(h * LANES, (h + 1) * LANES)
        qo_ref[:, sl] = (_rope_apply(q_ref[:, sl].astype(F32), tq, hq)
                         * (ATT_SCALE * LOG2_E)).astype(BF16)
        ko_ref[:, sl] = _rope_apply(k_ref[:, sl].astype(F32), tq, hq).astype(BF16)
    for p in range(ZI_IK // LANES):
        sl = slice(p * LANES, (p + 1) * LANES)
        iqo_ref[:, sl] = _rope_apply(zi_ref[:, sl], ti, hi).astype(BF16)
    grp = zi_ref[:, ZI_IK:ZI_IK + LANES]
    ik = _rope_apply(grp, ti, hi)
    lane = lax.broadcasted_iota(jnp.int32, (T, LANES), 1)
    iko_ref[...] = jnp.where(lane < IDX_HEAD_DIM, ik, pltpu.roll(ik, IDX_HEAD_DIM, 1)).astype(BF16)
    iwo_ref[...] = grp * (IDX_W_SCALE * IDX_SCALE)


def _lane_inv_freq(head_dim):
    rd = head_dim // ROPE_FRACTION
    half = rd // 2
    inv_freq = jnp.power(ROPE_THETA, -jnp.arange(half, dtype=F32) * (2.0 / rd))
    lane = np.arange(LANES) % head_dim
    idx = np.where(lane < rd, lane % half, 0)
    return jnp.where(jnp.asarray(lane < rd), inv_freq[idx], 0.0).reshape(1, LANES)


def _att_prep(zb, zi):
    L = zb.shape[0]
    T = min(256, L)
    row = lambda i: (i, 0)
    return pl.pallas_call(
        functools.partial(_attprep_kernel, T=T),
        grid=(L // T,),
        in_specs=[pl.BlockSpec((T, D_ATT), lambda i: (i, 2)),
                  pl.BlockSpec((T, D_ATT), lambda i: (i, 3)),
                  pl.BlockSpec((T, ZI_WIDTH), row),
                  pl.BlockSpec((1, LANES), lambda i: (0, 0)),
                  pl.BlockSpec((1, LANES), lambda i: (0, 0))],
        out_specs=[pl.BlockSpec((T, D_ATT), row), pl.BlockSpec((T, D_ATT), row),
                   pl.BlockSpec((T, ZI_IK), row), pl.BlockSpec((T, LANES), row),
                   pl.BlockSpec((T, LANES), row)],
        out_shape=[jax.ShapeDtypeStruct((L, D_ATT), BF16), jax.ShapeDtypeStruct((L, D_ATT), BF16),
                   jax.ShapeDtypeStruct((L, ZI_IK), BF16), jax.ShapeDtypeStruct((L, LANES), BF16),
                   jax.ShapeDtypeStruct((L, LANES), F32)],
        compiler_params=_params(("parallel",)),
        name="att_prep",
    )(zb, zb, zi, _lane_inv_freq(ATT_HEAD_DIM), _lane_inv_freq(IDX_HEAD_DIM))


def _dsa_kernel(q_ref, iq_ref, iw_ref, ik_ref, k_ref, v_ref, o_ref,
                key_ref, wb_ref, iqm_ref, thr_ref, m_ref, l_ref, acc_ref, *, Q, S, RB, topk):
    i = pl.program_id(0)
    j = pl.program_id(1)
    q_end = (i + 1) * Q

    @pl.when(j == 0)
    def _select():
        lane = lax.broadcasted_iota(jnp.int32, (Q, LANES), 1)
        lo = lane < IDX_HEAD_DIM
        zero = jnp.zeros((Q, LANES), BF16)
        for p in range(IDX_HEADS // 2):
            x = iq_ref[:, p * LANES:(p + 1) * LANES]
            iqm_ref[2 * p] = jnp.where(lo, x, zero)
            iqm_ref[2 * p + 1] = jnp.where(lo, zero, x)
        w = iw_ref[...]
        for h in range(IDX_HEADS):
            wb_ref[h] = jnp.broadcast_to(w[:, ZI_IW_LANE + h:ZI_IW_LANE + h + 1], (Q, LANES))
        row = i * Q + lax.broadcasted_iota(jnp.int32, (Q, S), 0)

        def score_chunk(c, carry):
            off = pl.multiple_of(c * S, S)
            ikc = ik_ref[pl.ds(off, S), :]
            sc = jnp.zeros((Q, S), F32)
            for h in range(IDX_HEADS):
                lg = lax.dot_general(iqm_ref[h], ikc, NT_DIMS, preferred_element_type=F32)
                sc = sc + jnp.maximum(lg, 0.0) * jnp.tile(wb_ref[h], (1, S // LANES))
            sc = jnp.where(sc == 0.0, 0.0, sc)
            bits = pltpu.bitcast(sc, jnp.int32)
            key = bits ^ ((bits >> 31) & 0x7FFFFFFF)
            col = off + lax.broadcasted_iota(jnp.int32, (Q, S), 1)
            key_ref[:, pl.ds(off, S)] = jnp.where(col <= row, key, INT_MIN)
            return carry

        nchunk = (q_end + S - 1) // S
        lax.fori_loop(0, nchunk, score_chunk, 0)

        def row_block(rb, carry):
            rows = pl.ds(pl.multiple_of(rb * RB, RB), RB)

            def bit_step(b, prefix):
                cand_u = prefix | jnp.left_shift(jnp.int32(1), 31 - b)
                cand = jnp.broadcast_to(cand_u ^ INT_MIN, (RB, LANES))

                def count(c, cnt):
                    off = pl.multiple_of(c * S, S)
                    for u in range(S // LANES):
                        kc = key_ref[rows, pl.ds(off + u * LANES, LANES)]
                        cnt = cnt + jnp.where(kc >= cand, 1.0, 0.0)
                    return cnt

                cnt = lax.fori_loop(0, nchunk, count, jnp.zeros((RB, LANES), F32))
                tot = jnp.sum(cnt, axis=1, keepdims=True)
                return jnp.where(tot >= float(topk), cand_u, prefix)

            prefix = lax.fori_loop(0, 32, bit_step, jnp.zeros((RB, 1), jnp.int32))
            thr = jnp.maximum(prefix ^ INT_MIN, INT_MIN + 1)
            thr_ref[rows, :] = jnp.broadcast_to(thr, (RB, LANES))
            return carry

        lax.fori_loop(0, Q // RB, row_block, 0)
        m_ref[...] = jnp.full(m_ref.shape, NEG_BIG, F32)
        l_ref[...] = jnp.zeros(l_ref.shape, F32)
        acc_ref[...] = jnp.zeros(acc_ref.shape, F32)

    @pl.when(j * S < q_end)
    def _attend():
        keys = key_ref[:, pl.ds(pl.multiple_of(j * S, S), S)]
        bias = jnp.where(keys >= jnp.tile(thr_ref[...], (1, S // LANES)), 0.0, NEG_BIG)
        for h in range(ATT_HEADS):
            sl = slice(h * LANES, (h + 1) * LANES)
            s = lax.dot_general(q_ref[:, sl], k_ref[:, sl], NT_DIMS, preferred_element_type=F32)
            s = s + bias
            m_prev = m_ref[h]
            m_new = jnp.maximum(m_prev, jnp.max(s, axis=1, keepdims=True))
            alpha = jnp.exp2(m_prev - m_new)
            p = jnp.exp2(s - jnp.tile(m_new, (1, S // LANES)))
            l_ref[h] = alpha * l_ref[h] + jnp.sum(p, axis=1, keepdims=True)
            acc_ref[:, sl] = alpha * acc_ref[:, sl] + _dot(p.astype(BF16), v_ref[:, sl])
            m_ref[h] = m_new

    @pl.when(j == pl.num_programs(1) - 1)
    def _finish():
        for h in range(ATT_HEADS):
            sl = slice(h * LANES, (h + 1) * LANES)
            o_ref[:, sl] = (acc_ref[:, sl] / l_ref[h]).astype(o_ref.dtype)


def _dsa_attention(qr, iqr, iws, ik2, kr, zb):
    L = qr.shape[0]
    Q = min(512, L)
    S = min(512, L)
    RB = 128
    topk = min(TOPK_MAX, L // 4)
    nk = L // S

    def kv_block(i, j):
        return jnp.minimum(j, ((i + 1) * Q - 1) // S)

    return pl.pallas_call(
        functools.partial(_dsa_kernel, Q=Q, S=S, RB=RB, topk=topk),
        grid=(L // Q, nk),
        in_specs=[pl.BlockSpec((Q, D_ATT), lambda i, j: (i, 0)),
                  pl.BlockSpec((Q, ZI_IK), lambda i, j: (i, 0)),
                  pl.BlockSpec((Q, LANES), lambda i, j: (i, 0)),
                  pl.BlockSpec((L, LANES), lambda i, j: (0, 0)),
                  pl.BlockSpec((S, D_ATT), lambda i, j: (kv_block(i, j), 0)),
                  pl.BlockSpec((S, D_ATT), lambda i, j: (kv_block(i, j), 4))],
        out_specs=pl.BlockSpec((Q, D_ATT), lambda i, j: (i, 0)),
        out_shape=jax.ShapeDtypeStruct((L, D_ATT), BF16),
        scratch_shapes=[pltpu.VMEM((Q, L), jnp.int32),
                        pltpu.VMEM((IDX_HEADS, Q, LANES), F32),
                        pltpu.VMEM((IDX_HEADS, Q, LANES), BF16),
                        pltpu.VMEM((Q, LANES), jnp.int32),
                        pltpu.VMEM((ATT_HEADS, Q, LANES), F32),
                        pltpu.VMEM((ATT_HEADS, Q, LANES), F32),
                        pltpu.VMEM((Q, D_ATT), F32)],
        compiler_params=_params(("parallel", "arbitrary")),
        name="dsa_attention",
    )(qr, iqr, iws, ik2, kr, zb)


def _head_sum(x, e):
    cols = []
    for p in range(x.shape[1] // LANES):
        cols.append(_dot_exact_rhs(x[:, p * LANES:(p + 1) * LANES], e))
    return jnp.concatenate(cols, axis=1)


def _rwkv_prep_kernel(*refs, T, has_vres):
    if has_vres:
        (z_ref, halo_ref, mu_ref, w0_ref, w2_ref, a0_ref, a2_ref, g2_ref, kk_ref, ka_ref, rk_ref,
         vf_ref, vup_ref, vb_ref,
         rt_ref, at_ref, bt_ref, kt_ref, v_ref, bonus_ref, g_ref, pc_ref) = refs
    else:
        (z_ref, halo_ref, mu_ref, w0_ref, w2_ref, a0_ref, a2_ref, g2_ref, kk_ref, ka_ref, rk_ref,
         rt_ref, at_ref, bt_ref, kt_ref, v_ref, bonus_ref, g_ref, pc_ref) = refs
    i = pl.program_id(0)
    D = D_RWKV

    def shifted(lo, hi):
        z = z_ref[:, lo:hi]
        first = jnp.where(i > 0, halo_ref[7:8, lo:hi], 0.0)
        rows = lax.broadcasted_iota(jnp.int32, z.shape, 0)
        prev = jnp.where(rows == 0, first, pltpu.roll(z, 1, 0))
        return z + (prev - z) * mu_ref[:, lo:hi]

    r = shifted(0, D)
    kraw = shifted(D, 2 * D)
    v = shifted(2 * D, 3 * D)
    xw = shifted(ZR_XW, ZR_XW + LANES)
    xa = shifted(*ZR_XA_WIN)
    xg = shifted(*ZR_XG_WIN)
    if has_vres:
        xv = shifted(*ZR_XV_WIN)
        mix = _sigmoid(vb_ref[...] + _dot(xv.astype(BF16), vup_ref[...]))
        v = v + (vf_ref[...] - v) * mix
    v_ref[...] = v

    w_in = w0_ref[...] + _dot(jnp.tanh(xw).astype(BF16), w2_ref[...])
    w_log = -(jnp.maximum(-w_in, 0.0) + jnp.log(1.0 + jnp.exp(-jnp.abs(w_in)))) - 0.5
    logw = -jnp.exp(w_log)
    a_lr = _sigmoid(a0_ref[...] + _dot(xa.astype(BF16), a2_ref[...]))
    g_ref[...] = _dot(_sigmoid(xg).astype(BF16), g2_ref[...])

    li = lax.broadcasted_iota(jnp.int32, (LANES, LANES), 0) // RWKV_HEAD
    lj = lax.broadcasted_iota(jnp.int32, (LANES, LANES), 1) // RWKV_HEAD
    e_head = jnp.where(li == lj, 1.0, 0.0).astype(BF16)
    kk = kraw * kk_ref[...]
    norm = jnp.sqrt(_head_sum(kk * kk, e_head))
    kk = kk / jnp.maximum(norm, 1e-12)
    k = kraw * (1.0 + (a_lr - 1.0) * ka_ref[...])
    bonus_ref[...] = _head_sum(r * k * rk_ref[...], e_head) * v

    ti = lax.broadcasted_iota(jnp.int32, (T, T), 0)
    tj = lax.broadcasted_iota(jnp.int32, (T, T), 1)
    tri = jnp.where((ti // CHUNK == tj // CHUNK) & (tj <= ti), 1.0, 0.0).astype(BF16)
    cum = _dot_exact_lhs(tri, logw)
    p_in = jnp.exp(cum)
    p_out = jnp.exp(-cum)
    rt_ref[...] = r * p_in
    at_ref[...] = -kk * jnp.exp(cum - logw)
    bt_ref[...] = kk * a_lr * p_out
    kt_ref[...] = k * p_out
    for c in range(T // CHUNK):
        pc_ref[c] = p_in[(c + 1) * CHUNK - 1:(c + 1) * CHUNK, :]


def _rwkv_prep(zr, mu, w0, w2, a0, a2, g2, k_k, k_a, r_k, vres):
    L = zr.shape[0]
    T = min(128, L)
    D = D_RWKV
    row = lambda i: (i, 0)
    const = lambda i: (0, 0)
    vec = lambda a: a.reshape(1, -1)
    has_vres = vres is not None
    ins = [zr, zr, vec(mu), vec(w0), w2, vec(a0), a2, g2, vec(k_k), vec(k_a), vec(r_k)]
    in_specs = [pl.BlockSpec((T, ZR_WIDTH), row),
                pl.BlockSpec((8, ZR_WIDTH), lambda i: (jnp.maximum(i * (T // 8) - 1, 0), 0)),
                pl.BlockSpec((1, ZR_WIDTH), const),
                pl.BlockSpec((1, D), const), pl.BlockSpec(w2.shape, const),
                pl.BlockSpec((1, D), const), pl.BlockSpec(a2.shape, const),
                pl.BlockSpec(g2.shape, const),
                pl.BlockSpec((1, D), const), pl.BlockSpec((1, D), const), pl.BlockSpec((1, D), const)]
    if has_vres:
        v_first, v_up, v_bias = vres
        ins += [v_first, v_up, vec(v_bias)]
        in_specs += [pl.BlockSpec((T, D), row), pl.BlockSpec((LANES, D), const),
                     pl.BlockSpec((1, D), const)]
    big = jax.ShapeDtypeStruct((L, D), F32)
    return pl.pallas_call(
        functools.partial(_rwkv_prep_kernel, T=T, has_vres=has_vres),
        grid=(L // T,),
        in_specs=in_specs,
        out_specs=[pl.BlockSpec((T, D), row)] * 7
                  + [pl.BlockSpec((T // CHUNK, 1, D), lambda i: (i, 0, 0))],
        out_shape=[big] * 7 + [jax.ShapeDtypeStruct((L // CHUNK, 1, D), F32)],
        compiler_params=_params(("parallel",)),
        name="rwkv_prep",
    )(*ins)


def _stack2(x, lo):
    zero = jnp.zeros_like(x)
    return jnp.concatenate([jnp.where(lo, x, zero), jnp.where(lo, zero, x)], axis=0)


def _scan_a_kernel(rt_ref, at_ref, bt_ref, kt_ref, v_ref, pc_ref, rh_ref, y0_ref, g_ref, h_ref,
                   *, CB, UNROLL):
    C = CHUNK
    lane = lax.broadcasted_iota(jnp.int32, (C, LANES), 1)
    lo = lane < RWKV_HEAD
    spos = lane % RWKV_HEAD
    tpos = lax.broadcasted_iota(jnp.int32, (C, LANES), 0)
    strict = spos < tpos
    incl = spos <= tpos
    eye_c = jnp.where(spos == tpos, 1.0, 0.0)
    ri = lax.broadcasted_iota(jnp.int32, (LANES, LANES), 0)
    ci = lax.broadcasted_iota(jnp.int32, (LANES, LANES), 1)
    same_head = (ri // RWKV_HEAD) == (ci // RWKV_HEAD)
    diag = ri == ci

    def stk(x):
        return _stack2(x, lo).astype(BF16)

    nt = lambda a, b: lax.dot_general(a, b, NT_DIMS, preferred_element_type=F32)
    tn = lambda a, b: lax.dot_general(a, b, TN_DIMS, preferred_element_type=F32)

    def each(fn, *cols):
        return [fn(*args) for args in zip(*cols)]

    def body(cg, carry):
        cs = [cg * UNROLL + u for u in range(UNROLL)]
        rows = [pl.ds(pl.multiple_of(c * C, C), C) for c in cs]
        rt = [rt_ref[r, :] for r in rows]
        at = [at_ref[r, :] for r in rows]
        bt = [bt_ref[r, :] for r in rows]
        kt = [kt_ref[r, :] for r in rows]
        v = [v_ref[r, :] for r in rows]
        pc = [pc_ref[c] for c in cs]
        at_b = each(lambda x: x.astype(BF16), at)
        rt_b = each(lambda x: x.astype(BF16), rt)
        bs = each(stk, bt)
        ks = each(stk, kt)
        vs = each(stk, v)
        m_ab = each(lambda a, b: jnp.where(strict, nt(a, b), 0.0), at_b, bs)
        m_ak = each(lambda a, b: jnp.where(strict, nt(a, b), 0.0), at_b, ks)
        m_rb = each(lambda a, b: jnp.where(incl, nt(a, b), 0.0), rt_b, bs)
        m_rk = each(lambda a, b: jnp.where(incl, nt(a, b), 0.0), rt_b, ks)
        mv = each(lambda m, x: _dot(m.astype(BF16), x), m_ak, vs)
        pw = m_ab
        inv = each(lambda m: eye_c + m, pw)
        for _ in range(5):
            pw = each(lambda x: _dot(x.astype(BF16), stk(x)), pw)
            inv = each(lambda t, x: t + _dot(t.astype(BF16), stk(x)), inv, pw)
        inv_b = each(lambda x: x.astype(BF16), inv)
        w = each(lambda t, x: _dot(t, stk(x)), inv_b, at)
        u0 = each(lambda t, x: _dot(t, stk(x)), inv_b, mv)
        m_rb_b = each(lambda x: x.astype(BF16), m_rb)
        rh = each(lambda r, m, x: r + _dot(m, stk(x)), rt, m_rb_b, w)
        y0 = each(lambda m, x, m2, x2: _dot(m, stk(x)) + _dot(m2.astype(BF16), x2),
                  m_rb_b, u0, m_rk, vs)
        btp = each(lambda x, p: (x * p).astype(BF16), bt, pc)
        ktp = each(lambda x, p: (x * p).astype(BF16), kt, pc)
        gm = each(lambda b, x, p: jnp.where(diag, jnp.broadcast_to(p, (LANES, LANES)), 0.0)
                  + jnp.where(same_head, tn(b, x.astype(BF16)), 0.0), btp, w, pc)
        hm = each(lambda b, x, k, y: jnp.where(same_head, tn(b, x.astype(BF16))
                                               + tn(k, y.astype(BF16)), 0.0), btp, u0, ktp, v)
        for u in range(UNROLL):
            rh_ref[rows[u], :] = rh[u]
            y0_ref[rows[u], :] = y0[u]
            g_ref[cs[u], 0] = gm[u]
            h_ref[cs[u], 0] = hm[u]
        return carry

    lax.fori_loop(0, CB // UNROLL, body, 0)


def _scan_a(rt, at, bt, kt, v, pc):
    L, D = rt.shape
    NP = D // LANES
    NC = L // CHUNK
    CB = min(8, NC)
    blk = pl.BlockSpec((CB * CHUNK, LANES), lambda ci, p: (ci, p))
    gh = pl.BlockSpec((CB, 1, LANES, LANES), lambda ci, p: (ci, p, 0, 0))
    return pl.pallas_call(
        functools.partial(_scan_a_kernel, CB=CB, UNROLL=min(8, CB)),
        grid=(NC // CB, NP),
        in_specs=[blk] * 5 + [pl.BlockSpec((CB, 1, LANES), lambda ci, p: (ci, 0, p))],
        out_specs=[blk, blk, gh, gh],
        out_shape=[jax.ShapeDtypeStruct((L, D), F32), jax.ShapeDtypeStruct((L, D), F32),
                   jax.ShapeDtypeStruct((NC, NP, LANES, LANES), F32),
                   jax.ShapeDtypeStruct((NC, NP, LANES, LANES), F32)],
        compiler_params=_params(("parallel", "parallel")),
        name="rwkv_chunk_local",
    )(rt, at, bt, kt, v, pc)


def _scan_b_kernel(rh_ref, y0_ref, g_ref, h_ref, bonus_ref, gate_ref, lg_ref, lb_ref, o_ref,
                   st_ref, *, CB, PP):
    ci = pl.program_id(1)

    @pl.when(ci == 0)
    def _():
        st_ref[...] = jnp.zeros(st_ref.shape, F32)

    li = lax.broadcasted_iota(jnp.int32, (LANES, LANES), 0) // RWKV_HEAD
    lj = lax.broadcasted_iota(jnp.int32, (LANES, LANES), 1) // RWKV_HEAD
    e_mean = jnp.where(li == lj, 1.0 / RWKV_HEAD, 0.0).astype(BF16)

    def body(c, carry):
        rows = pl.ds(pl.multiple_of(c * CHUNK, CHUNK), CHUNK)
        sls = [slice(p * LANES, (p + 1) * LANES) for p in range(PP)]

        def each(fn, *cols):
            return [fn(*args) for args in zip(*cols)]

        st = [st_ref[p] for p in range(PP)]
        y = each(lambda sl, s: _dot3(rh_ref[rows, sl], s) + y0_ref[rows, sl], sls, st)
        st_new = each(lambda p, s: _dot3(g_ref[c, p], s) + h_ref[c, p], list(range(PP)), st)
        for p in range(PP):
            st_ref[p] = st_new[p]
        mean = each(lambda x: _dot_exact_rhs(x, e_mean), y)
        d = each(lambda x, m: x - m, y, mean)
        var = each(lambda x: _dot_exact_rhs(x * x, e_mean), d)
        for p in range(PP):
            sl = sls[p]
            yn = d[p] * lax.rsqrt(var[p] + RWKV_GN_EPS) * lg_ref[:, sl] + lb_ref[:, sl]
            o_ref[rows, sl] = ((yn + bonus_ref[rows, sl]) * gate_ref[rows, sl]).astype(o_ref.dtype)
        return carry

    lax.fori_loop(0, CB, body, 0)


def _scan_b(rh, y0, g, h, bonus, gate, ln_g, ln_b):
    L, D = rh.shape
    NP = D // LANES
    NC = L // CHUNK
    CB = min(8, NC)
    PP = 8
    blk = pl.BlockSpec((CB * CHUNK, PP * LANES), lambda pg, ci: (ci, pg))
    gh = pl.BlockSpec((CB, PP, LANES, LANES), lambda pg, ci: (ci, pg, 0, 0))
    vec = pl.BlockSpec((1, PP * LANES), lambda pg, ci: (0, pg))
    return pl.pallas_call(
        functools.partial(_scan_b_kernel, CB=CB, PP=PP),
        grid=(NP // PP, NC // CB),
        in_specs=[blk, blk, gh, gh, blk, blk, vec, vec],
        out_specs=blk,
        out_shape=jax.ShapeDtypeStruct((L, D), BF16),
        scratch_shapes=[pltpu.VMEM((PP, LANES, LANES), F32)],
        compiler_params=_params(("parallel", "arbitrary")),
        name="rwkv_state_scan",
    )(rh, y0, g, h, bonus, gate, ln_g.reshape(1, -1), ln_b.reshape(1, -1))


def _ct_kernel(w_ref, o_ref):
    o_ref[...] = w_ref[0].T.astype(BF16)


def _ct_patch_kernel(w_ref, patch_ref, o_ref, *, first_patch_row):
    w = w_ref[0]
    rows = lax.broadcasted_iota(jnp.int32, w.shape, 0)
    last = pl.program_id(0) == pl.num_programs(0) - 1
    w = jnp.where(last & (rows >= first_patch_row), patch_ref[...], w)
    o_ref[...] = w.T.astype(BF16)


def _cast_transpose(w_t, l, n_cols, tn, row_of_tile, name, patch=None, first_patch_row=0):
    K = w_t.shape[2]
    in_specs = [pl.BlockSpec((pl.Element(1), pl.Element(tn), pl.Element(K)),
                             lambda j: (l, pl.multiple_of(row_of_tile(j), 16), 0))]
    args = [w_t]
    body = _ct_kernel
    if patch is not None:
        in_specs.append(pl.BlockSpec((tn, K), lambda j: (0, 0)))
        args.append(patch)
        body = functools.partial(_ct_patch_kernel, first_patch_row=first_patch_row)
    return pl.pallas_call(
        body,
        grid=(n_cols // tn,),
        in_specs=in_specs,
        out_specs=pl.BlockSpec((K, tn), lambda j: (0, j)),
        out_shape=jax.ShapeDtypeStruct((K, n_cols), BF16),
        compiler_params=_params(("parallel",)),
        name=name,
    )(*args)


def _pack_in_proj(w_t, l, vdown):
    tn = 512
    n_plain = IN_IDX // tn
    w_b = _cast_transpose(
        w_t, l, ZB_WIDTH, tn,
        lambda j: jnp.where(j < n_plain, j * tn, IN_GATE + (j - n_plain) * tn), "pack_w_b")
    n_own = ZR_XV - (ZR_WIDTH - tn)
    patch = jnp.pad(vdown.T, ((n_own, 0), (0, 0)))
    w_r = _cast_transpose(w_t, l, ZR_WIDTH, tn, lambda j: IN_RWKV + j * tn, "pack_w_r",
                          patch=patch, first_patch_row=n_own)
    w_i = _cast_transpose(w_t, l, ZI_WIDTH, 256, lambda j: IN_IDX + j * 256, "pack_w_i")
    return w_b, w_r, w_i


def _rows_at(w, start, height):
    return jnp.pad(w, ((start, height - start - w.shape[0]), (0, 0)))


def kernel(x, w_in, norm_mix, dw_weight, dw_bias, conv_ln_g, conv_ln_b, w_conv_out, w_att_out, rwkv_mu, rwkv_w0, rwkv_w2, rwkv_a0, rwkv_a2, rwkv_g2, rwkv_k_k, rwkv_k_a, rwkv_r_k, rwkv_ln_g, rwkv_ln_b, vres_down, vres_mu, vres_up, vres_bias, w_rwkv_out, w_out, norm_ffn, w_ffn_gate, w_ffn_up, w_ffn_down, norm_final):
    B, L, D = x.shape
    assert B == 1 and D == D_MODEL and L % 256 == 0
    depth = w_in.shape[0]
    xs = x.reshape(L, D)
    assert w_in.shape[1:] == (D_MODEL, D_IN)
    w_t = jnp.swapaxes(w_in, 1, 2)
    wc_b, wa_b, wr_b, wo_b = (w.astype(BF16) for w in (w_conv_out, w_att_out, w_rwkv_out, w_out))
    wg_b, wu_b, wd_b = (w.astype(BF16) for w in (w_ffn_gate, w_ffn_up, w_ffn_down))
    v_first = None
    for l in range(depth):
        vdown = vres_down[l - 1] if l > 0 else jnp.zeros((D, LORA_MV), F32)
        w_b, w_r, w_i = _pack_in_proj(w_t, l, vdown)
        mu_p = jnp.concatenate([rwkv_mu[l], vres_mu[l - 1] if l > 0 else jnp.zeros((LORA_MV,), F32)])
        w2_p = _rows_at(rwkv_w2[l], 0, LANES).astype(BF16)
        a2_p = _rows_at(rwkv_a2[l], ZR_XA - ZR_XA_WIN[0], ZR_XA_WIN[1] - ZR_XA_WIN[0]).astype(BF16)
        g2_p = _rows_at(rwkv_g2[l], ZR_XG - ZR_XG_WIN[0], ZR_XG_WIN[1] - ZR_XG_WIN[0]).astype(BF16)

        h = _rms_norm(xs, norm_mix[l], BF16)
        zb = _matmul(h, w_b, BF16, 1024, 512, "in_proj_bf16")
        zr = _matmul(h, w_r, F32, 1024, 512, "in_proj_rwkv")
        zi = _matmul(h, w_i, F32, 1024, 256, "in_proj_index")

        a_mix = _conformer_conv(zb, dw_weight[l], dw_bias[l], conv_ln_g[l], conv_ln_b[l])

        qr, kr, iqr, ik2, iws = _att_prep(zb, zi)
        b_mix = _dsa_attention(qr, iqr, iws, ik2, kr, zb)

        vres = None
        if l > 0:
            vup_p = _rows_at(vres_up[l - 1], ZR_XV - ZR_XV_WIN[0], LANES).astype(BF16)
            vres = (v_first, vup_p, vres_bias[l - 1])
        rt, at, bt, kt, v_rwkv, bonus, gate, pc = _rwkv_prep(
            zr, mu_p, rwkv_w0[l], w2_p, rwkv_a0[l], a2_p, g2_p,
            rwkv_k_k[l], rwkv_k_a[l], rwkv_r_k[l].reshape(-1), vres)
        if l == 0:
            v_first = v_rwkv
        rh, y0, g_mat, h_mat = _scan_a(rt, at, bt, kt, v_rwkv, pc)
        c_mix = _scan_b(rh, y0, g_mat, h_mat, bonus, gate, rwkv_ln_g[l], rwkv_ln_b[l])

        merged = _merge(a_mix, b_mix, c_mix, wc_b, wa_b, wr_b, l, zb, 1024, 512)
        xs = _matmul_residual(merged, wo_b, l, xs, 1024, 512, "out_proj")

        h2 = _rms_norm(xs, norm_ffn[l], BF16)
        act = _ffn1(h2, wg_b, wu_b, l, 1024, 256)
        xs = _matmul_residual(act, wd_b, l, xs, 512, 256, "ffn_down")
    return _rms_norm(xs, norm_final, F32).reshape(B, L, D)
```

```python
import functools

import jax
import jax.numpy as jnp
import numpy as np
from jax import lax
from jax.experimental import pallas as pl
from jax.experimental.pallas import tpu as pltpu

F32 = jnp.float32
BF16 = jnp.bfloat16

D_MODEL = 4096
RMS_EPS = 1e-6
LN_EPS = 1e-5
D_CONV = D_MODEL // 4
CONV_WIDTH = 31
ATT_HEADS = 8
ATT_HEAD_DIM = 128
D_ATT = ATT_HEADS * ATT_HEAD_DIM
ATT_SCALE = ATT_HEAD_DIM ** -0.5
LOG2_E = 1.4426950408889634
IDX_HEADS = 16
IDX_HEAD_DIM = 64
IDX_SCALE = IDX_HEAD_DIM ** -0.5
IDX_W_SCALE = IDX_HEADS ** -0.5
TOPK_MAX = 256
ROPE_THETA = 500000.0
ROPE_FRACTION = 4
D_RWKV = D_MODEL // 2
RWKV_HEAD = 64
LORA_DECAY = 96
LORA_AAA = 96
LORA_MV = 64
LORA_GATE = 256
RWKV_GN_EPS = 64e-5
D_FF = ((8 * D_MODEL + 3 * 256 - 1) // (3 * 256)) * 256

LANES = 128
CHUNK = 64
IN_ATT = 2 * D_CONV
IN_IDX = IN_ATT + 3 * D_ATT
IN_RWKV = IN_IDX + IDX_HEADS * IDX_HEAD_DIM + IDX_HEAD_DIM + IDX_HEADS
IN_GATE = IN_RWKV + 3 * D_RWKV + LORA_DECAY + LORA_AAA + LORA_GATE
D_IN = IN_GATE + 3 * D_MODEL
ZR_XW = 3 * D_RWKV
ZR_XA = ZR_XW + LORA_DECAY
ZR_XG = ZR_XA + LORA_AAA
ZR_XV = ZR_XG + LORA_GATE
ZR_WIDTH = ZR_XV + LORA_MV
ZR_XA_WIN = (ZR_XW, ZR_XW + 2 * LANES)
ZR_XG_WIN = (ZR_XW + LANES, ZR_WIDTH)
ZR_XV_WIN = (ZR_WIDTH - LANES, ZR_WIDTH)
ZI_IK = IDX_HEADS * IDX_HEAD_DIM
ZI_IW_LANE = IDX_HEAD_DIM
ZI_WIDTH = ZI_IK + 2 * LANES
ZB_GATE = 2 * D_CONV + 3 * D_ATT
ZB_WIDTH = ZB_GATE + 3 * D_MODEL
VMEM_LIMIT = 56 * 1024 * 1024
INT_MIN = -2147483648
NEG_BIG = -1e30

NT_DIMS = (((1,), (1,)), ((), ()))
TN_DIMS = (((0,), (0,)), ((), ()))


def _params(sem, vmem=VMEM_LIMIT):
    return pltpu.CompilerParams(dimension_semantics=sem, vmem_limit_bytes=vmem)


def _dot(a, b):
    return jnp.dot(a, b, preferred_element_type=F32)


def _split2(x):
    hi = x.astype(BF16)
    lo = (x - hi.astype(F32)).astype(BF16)
    return hi, lo


def _dot_exact_rhs(x, e):
    h1 = x.astype(BF16)
    r1 = x - h1.astype(F32)
    h2 = r1.astype(BF16)
    h3 = (r1 - h2.astype(F32)).astype(BF16)
    return _dot(h1, e) + _dot(h2, e) + _dot(h3, e)


def _dot_exact_lhs(e, x):
    h1 = x.astype(BF16)
    r1 = x - h1.astype(F32)
    h2 = r1.astype(BF16)
    h3 = (r1 - h2.astype(F32)).astype(BF16)
    return _dot(e, h1) + _dot(e, h2) + _dot(e, h3)


def _dot3(a, b):
    ah, al = _split2(a)
    bh, bl = _split2(b)
    return _dot(ah, bh) + _dot(ah, bl) + _dot(al, bh)


def _sigmoid(x):
    return 1.0 / (1.0 + jnp.exp(-x))


def _rms_kernel(x_ref, g_ref, o_ref):
    x = x_ref[...]
    y = x * lax.rsqrt(jnp.mean(x * x, axis=-1, keepdims=True) + RMS_EPS)
    o_ref[...] = (y * g_ref[...]).astype(o_ref.dtype)


def _rms_norm(x, g, out_dtype):
    L, D = x.shape
    tr = min(256, L)
    return pl.pallas_call(
        _rms_kernel,
        grid=(L // tr,),
        in_specs=[pl.BlockSpec((tr, D), lambda i: (i, 0)),
                  pl.BlockSpec((1, D), lambda i: (0, 0))],
        out_specs=pl.BlockSpec((tr, D), lambda i: (i, 0)),
        out_shape=jax.ShapeDtypeStruct((L, D), out_dtype),
        compiler_params=_params(("parallel",)),
        name="rms_norm",
    )(x, g.reshape(1, D))


def _mm_kernel(a_ref, b_ref, o_ref):
    o_ref[...] = _dot(a_ref[...], b_ref[...]).astype(o_ref.dtype)


def _matmul(a, b, out_dtype, tm, tn, name):
    M, K = a.shape
    N = b.shape[1]
    tm = min(tm, M)
    return pl.pallas_call(
        _mm_kernel,
        grid=(M // tm, N // tn),
        in_specs=[pl.BlockSpec((tm, K), lambda i, j: (i, 0)),
                  pl.BlockSpec((K, tn), lambda i, j: (0, j))],
        out_specs=pl.BlockSpec((tm, tn), lambda i, j: (i, j)),
        out_shape=jax.ShapeDtypeStruct((M, N), out_dtype),
        compiler_params=_params(("parallel", "parallel")),
        name=name,
    )(a, b)


def _mm_res_kernel(a_ref, b_ref, x_ref, o_ref):
    o_ref[...] = x_ref[...] + _dot(a_ref[...], b_ref[...])


def _layer_weight_spec(w, l, tn):
    return pl.BlockSpec((None, w.shape[1], tn), lambda i, j: (l, 0, j))


def _matmul_residual(a, b, l, x, tm, tn, name):
    M, K = a.shape
    N = b.shape[2]
    tm = min(tm, M)
    return pl.pallas_call(
        _mm_res_kernel,
        grid=(M // tm, N // tn),
        in_specs=[pl.BlockSpec((tm, K), lambda i, j: (i, 0)),
                  _layer_weight_spec(b, l, tn),
                  pl.BlockSpec((tm, tn), lambda i, j: (i, j))],
        out_specs=pl.BlockSpec((tm, tn), lambda i, j: (i, j)),
        out_shape=jax.ShapeDtypeStruct((M, N), F32),
        compiler_params=_params(("parallel", "parallel")),
        name=name,
    )(a, b, x)


def _ffn1_kernel(h_ref, wg_ref, wu_ref, o_ref):
    h = h_ref[...]
    g = _dot(h, wg_ref[...])
    u = _dot(h, wu_ref[...])
    o_ref[...] = (g * _sigmoid(g) * u).astype(o_ref.dtype)


def _ffn1(h, wg, wu, l, tm, tn):
    M, K = h.shape
    N = wg.shape[2]
    tm = min(tm, M)
    return pl.pallas_call(
        _ffn1_kernel,
        grid=(M // tm, N // tn),
        in_specs=[pl.BlockSpec((tm, K), lambda i, j: (i, 0)),
                  _layer_weight_spec(wg, l, tn),
                  _layer_weight_spec(wu, l, tn)],
        out_specs=pl.BlockSpec((tm, tn), lambda i, j: (i, j)),
        out_shape=jax.ShapeDtypeStruct((M, N), BF16),
        compiler_params=_params(("parallel", "parallel")),
        name="ffn_gate_up",
    )(h, wg, wu)


def _merge_kernel(a_ref, b_ref, c_ref, wa_ref, wb_ref, wc_ref, g0_ref, g1_ref, g2_ref, o_ref):
    ya = _dot(a_ref[...], wa_ref[...])
    yb = _dot(b_ref[...], wb_ref[...])
    yc = _dot(c_ref[...], wc_ref[...])
    m = (_sigmoid(g0_ref[...].astype(F32)) * ya + _sigmoid(g1_ref[...].astype(F32)) * yb
         + _sigmoid(g2_ref[...].astype(F32)) * yc)
    o_ref[...] = m.astype(o_ref.dtype)


def _merge(a, b, c, wa, wb, wc, l, zb, tm, tn):
    M = a.shape[0]
    tm = min(tm, M)
    g_off = ZB_GATE // tn
    g_stride = D_MODEL // tn
    return pl.pallas_call(
        _merge_kernel,
        grid=(M // tm, D_MODEL // tn),
        in_specs=[pl.BlockSpec((tm, D_CONV), lambda i, j: (i, 0)),
                  pl.BlockSpec((tm, D_ATT), lambda i, j: (i, 0)),
                  pl.BlockSpec((tm, D_RWKV), lambda i, j: (i, 0)),
                  _layer_weight_spec(wa, l, tn),
                  _layer_weight_spec(wb, l, tn),
                  _layer_weight_spec(wc, l, tn),
                  pl.BlockSpec((tm, tn), lambda i, j: (i, g_off + j)),
                  pl.BlockSpec((tm, tn), lambda i, j: (i, g_off + g_stride + j)),
                  pl.BlockSpec((tm, tn), lambda i, j: (i, g_off + 2 * g_stride + j))],
        out_specs=pl.BlockSpec((tm, tn), lambda i, j: (i, j)),
        out_shape=jax.ShapeDtypeStruct((M, D_MODEL), BF16),
        compiler_params=_params(("parallel", "parallel")),
        name="gated_merge",
    )(a, b, c, wa, wb, wc, zb, zb, zb)


def _conv_kernel(val_ref, gate_ref, hval_ref, hgate_ref, w_ref, b_ref, lg_ref, lb_ref, o_ref,
                 c_ref, *, T):
    i = pl.program_id(0)
    halo = 32
    hv = hval_ref[...].astype(F32)
    hg = hgate_ref[...].astype(F32)
    hc = hv * _sigmoid(hg)
    c_ref[0:halo, :] = jnp.where(i > 0, hc, 0.0)
    v = val_ref[...].astype(F32)
    g = gate_ref[...].astype(F32)
    c_ref[halo:halo + T, :] = v * _sigmoid(g)
    acc = jnp.zeros((T, D_CONV), F32) + b_ref[...]
    for j in range(CONV_WIDTH):
        off = halo - (CONV_WIDTH - 1) + j
        acc = acc + c_ref[off:off + T, :] * w_ref[j:j + 1, :]
    m = jnp.mean(acc, axis=-1, keepdims=True)
    d = acc - m
    var = jnp.mean(d * d, axis=-1, keepdims=True)
    y = d * lax.rsqrt(var + LN_EPS) * lg_ref[...] + lb_ref[...]
    o_ref[...] = (y * _sigmoid(y)).astype(o_ref.dtype)


def _conformer_conv(zb, dw_w, dw_b, ln_g, ln_b):
    L = zb.shape[0]
    T = min(256, L)
    hb = T // 32
    w_pad = jnp.zeros((32, D_CONV), F32).at[:CONV_WIDTH].set(dw_w)
    return pl.pallas_call(
        functools.partial(_conv_kernel, T=T),
        grid=(L // T,),
        in_specs=[pl.BlockSpec((T, D_CONV), lambda i: (i, 0)),
                  pl.BlockSpec((T, D_CONV), lambda i: (i, 1)),
                  pl.BlockSpec((32, D_CONV), lambda i: (jnp.maximum(i * hb - 1, 0), 0)),
                  pl.BlockSpec((32, D_CONV), lambda i: (jnp.maximum(i * hb - 1, 0), 1)),
                  pl.BlockSpec((32, D_CONV), lambda i: (0, 0)),
                  pl.BlockSpec((1, D_CONV), lambda i: (0, 0)),
                  pl.BlockSpec((1, D_CONV), lambda i: (0, 0)),
                  pl.BlockSpec((1, D_CONV), lambda i: (0, 0))],
        out_specs=pl.BlockSpec((T, D_CONV), lambda i: (i, 0)),
        out_shape=jax.ShapeDtypeStruct((L, D_CONV), BF16),
        scratch_shapes=[pltpu.VMEM((T + 32, D_CONV), F32)],
        compiler_params=_params(("parallel",)),
        name="conformer_conv",
    )(zb, zb, zb, zb, w_pad, dw_b.reshape(1, -1), ln_g.reshape(1, -1), ln_b.reshape(1, -1))


def _rope_tables(pos, inv_freq, head_dim):
    rd = head_dim // ROPE_FRACTION
    half = rd // 2
    ang = pos * inv_freq
    cos = jnp.cos(ang)
    sin = jnp.sin(ang)
    lane = lax.broadcasted_iota(jnp.int32, ang.shape, 1) % head_dim
    c = jnp.where(lane < rd, cos, 1.0)
    s_up = jnp.where(lane < half, -sin, 0.0)
    s_dn = jnp.where((lane >= half) & (lane < rd), sin, 0.0)
    return c, s_up, s_dn


def _rope_apply(x, tabs, half):
    c, s_up, s_dn = tabs
    return (x * c + pltpu.roll(x, LANES - half, 1) * s_up + pltpu.roll(x, half, 1) * s_dn)


def _attprep_kernel(q_ref, k_ref, zi_ref, fq_ref, fi_ref, qo_ref, ko_ref, iqo_ref, iko_ref,
                    iwo_ref, *, T):
    i = pl.program_id(0)
    pos = (i * T + lax.broadcasted_iota(jnp.int32, (T, LANES), 0)).astype(F32)
    tq = _rope_tables(pos, fq_ref[...], ATT_HEAD_DIM)
    ti = _rope_tables(pos, fi_ref[...], IDX_HEAD_DIM)
    hq = ATT_HEAD_DIM // ROPE_FRACTION // 2
    hi = IDX_HEAD_DIM // ROPE_FRACTION // 2
    for h in range(ATT_HEADS):
        sl = slice(h * LANES, (h + 1) * LANES)
        qo_ref[:, sl] = (_rope_apply(q_ref[:, sl].astype(F32), tq, hq)
                         * (ATT_SCALE * LOG2_E)).astype(BF16)
        ko_ref[:, sl] = _rope_apply(k_ref[:, sl].astype(F32), tq, hq).astype(BF16)
    for p in range(ZI_IK // LANES):
        sl = slice(p * LANES, (p + 1) * LANES)
        iqo_ref[:, sl] = _rope_apply(zi_ref[:, sl], ti, hi).astype(BF16)
    grp = zi_ref[:, ZI_IK:ZI_IK + LANES]
    ik = _rope_apply(grp, ti, hi)
    lane = lax.broadcasted_iota(jnp.int32, (T, LANES), 1)
    iko_ref[...] = jnp.where(lane < IDX_HEAD_DIM, ik, pltpu.roll(ik, IDX_HEAD_DIM, 1)).astype(BF16)
    iwo_ref[...] = grp * (IDX_W_SCALE * IDX_SCALE)


def _lane_inv_freq(head_dim):
    rd = head_dim // ROPE_FRACTION
    half = rd // 2
    inv_freq = jnp.power(ROPE_THETA, -jnp.arange(half, dtype=F32) * (2.0 / rd))
    lane = np.arange(LANES) % head_dim
    idx = np.where(lane < rd, lane % half, 0)
    return jnp.where(jnp.asarray(lane < rd), inv_freq[idx], 0.0).reshape(1, LANES)


def _att_prep(zb, zi):
    L = zb.shape[0]
    T = min(256, L)
    row = lambda i: (i, 0)
    return pl.pallas_call(
        functools.partial(_attprep_kernel, T=T),
        grid=(L // T,),
        in_specs=[pl.BlockSpec((T, D_ATT), lambda i: (i, 2)),
                  pl.BlockSpec((T, D_ATT), lambda i: (i, 3)),
                  pl.BlockSpec((T, ZI_WIDTH), row),
                  pl.BlockSpec((1, LANES), lambda i: (0, 0)),
                  pl.BlockSpec((1, LANES), lambda i: (0, 0))],
        out_specs=[pl.BlockSpec((T, D_ATT), row), pl.BlockSpec((T, D_ATT), row),
                   pl.BlockSpec((T, ZI_IK), row), pl.BlockSpec((T, LANES), row),
                   pl.BlockSpec((T, LANES), row)],
        out_shape=[jax.ShapeDtypeStruct((L, D_ATT), BF16), jax.ShapeDtypeStruct((L, D_ATT), BF16),
                   jax.ShapeDtypeStruct((L, ZI_IK), BF16), jax.ShapeDtypeStruct((L, LANES), BF16),
                   jax.ShapeDtypeStruct((L, LANES), F32)],
        compiler_params=_params(("parallel",)),
        name="att_prep",
    )(zb, zb, zi, _lane_inv_freq(ATT_HEAD_DIM), _lane_inv_freq(IDX_HEAD_DIM))


def _dsa_kernel(q_ref, iq_ref, iw_ref, ik_ref, k_ref, v_ref, o_ref,
                keyt_ref, wt_ref, iqm_ref, thr_ref, m_ref, l_ref, acc_ref, *, Q, S, SLAB, topk):
    i = pl.program_id(0)
    j = pl.program_id(1)
    q_end = (i + 1) * Q

    @pl.when(j == 0)
    def _select():
        lane = lax.broadcasted_iota(jnp.int32, (Q, LANES), 1)
        lo = lane < IDX_HEAD_DIM
        zero = jnp.zeros((Q, LANES), BF16)
        for p in range(IDX_HEADS // 2):
            x = iq_ref[:, p * LANES:(p + 1) * LANES]
            iqm_ref[2 * p] = jnp.where(lo, x, zero)
            iqm_ref[2 * p + 1] = jnp.where(lo, zero, x)
        wt_ref[...] = iw_ref[...].T
        qpos = i * Q + lax.broadcasted_iota(jnp.int32, (S, Q), 1)

        def score_chunk(c, carry):
            off = pl.multiple_of(c * S, S)
            ikc = ik_ref[pl.ds(off, S), :]
            sc = jnp.zeros((S, Q), F32)
            for h in range(IDX_HEADS):
                lg = lax.dot_general(ikc, iqm_ref[h], NT_DIMS, preferred_element_type=F32)
                sc = sc + jnp.maximum(lg, 0.0) * wt_ref[ZI_IW_LANE + h:ZI_IW_LANE + h + 1, :]
            sc = jnp.where(sc == 0.0, 0.0, sc)
            bits = pltpu.bitcast(sc, jnp.int32)
            key = bits ^ ((bits >> 31) & 0x7FFFFFFF)
            kpos = off + lax.broadcasted_iota(jnp.int32, (S, Q), 0)
            keyt_ref[pl.ds(off, S), :] = jnp.where(kpos <= qpos, key, INT_MIN)
            return carry

        nchunk = (q_end + S - 1) // S
        lax.fori_loop(0, nchunk, score_chunk, 0)

        def bit_step(b, prefix):
            cand_u = prefix | jnp.left_shift(jnp.int32(1), 31 - b)
            cand = jnp.broadcast_to(cand_u ^ INT_MIN, (SLAB, Q))

            def count(c, cnt):
                off = pl.multiple_of(c * S, S)
                for u in range(S // SLAB):
                    kc = keyt_ref[pl.ds(off + u * SLAB, SLAB), :]
                    cnt = cnt + jnp.where(kc >= cand, 1.0, 0.0)
                return cnt

            cnt = lax.fori_loop(0, nchunk, count, jnp.zeros((SLAB, Q), F32))
            tot = jnp.sum(cnt, axis=0, keepdims=True)
            return jnp.where(tot >= float(topk), cand_u, prefix)

        prefix = lax.fori_loop(0, 32, bit_step, jnp.zeros((1, Q), jnp.int32))
        thr = jnp.maximum(prefix ^ INT_MIN, INT_MIN + 1)
        thr_ref[...] = jnp.broadcast_to(thr, thr_ref.shape)
        m_ref[...] = jnp.full(m_ref.shape, NEG_BIG, F32)
        l_ref[...] = jnp.zeros(l_ref.shape, F32)
        acc_ref[...] = jnp.zeros(acc_ref.shape, F32)

    @pl.when(j * S < q_end)
    def _attend():
        keys = keyt_ref[pl.ds(pl.multiple_of(j * S, S), S), :]
        bias = jnp.where(keys >= thr_ref[0:1, :], 0.0, NEG_BIG).T
        for h in range(ATT_HEADS):
            sl = slice(h * LANES, (h + 1) * LANES)
            s = lax.dot_general(q_ref[:, sl], k_ref[:, sl], NT_DIMS, preferred_element_type=F32)
            s = s + bias
            m_prev = m_ref[h]
            m_new = jnp.maximum(m_prev, jnp.max(s, axis=1, keepdims=True))
            alpha = jnp.exp2(m_prev - m_new)
            p = jnp.exp2(s - jnp.tile(m_new, (1, S // LANES)))
            l_ref[h] = alpha * l_ref[h] + jnp.sum(p, axis=1, keepdims=True)
            acc_ref[:, sl] = alpha * acc_ref[:, sl] + _dot(p.astype(BF16), v_ref[:, sl])
            m_ref[h] = m_new

    @pl.when(j == pl.num_programs(1) - 1)
    def _finish():
        for h in range(ATT_HEADS):
            sl = slice(h * LANES, (h + 1) * LANES)
            o_ref[:, sl] = (acc_ref[:, sl] / l_ref[h]).astype(o_ref.dtype)


def _dsa_attention(qr, iqr, iws, ik2, kr, zb):
    L = qr.shape[0]
    Q = min(512, L)
    S = min(512, L)
    SLAB = 32
    topk = min(TOPK_MAX, L // 4)
    nk = L // S

    def kv_block(i, j):
        return jnp.minimum(j, ((i + 1) * Q - 1) // S)

    return pl.pallas_call(
        functools.partial(_dsa_kernel, Q=Q, S=S, SLAB=SLAB, topk=topk),
        grid=(L // Q, nk),
        in_specs=[pl.BlockSpec((Q, D_ATT), lambda i, j: (i, 0)),
                  pl.BlockSpec((Q, ZI_IK), lambda i, j: (i, 0)),
                  pl.BlockSpec((Q, LANES), lambda i, j: (i, 0)),
                  pl.BlockSpec((L, LANES), lambda i, j: (0, 0)),
                  pl.BlockSpec((S, D_ATT), lambda i, j: (kv_block(i, j), 0)),
                  pl.BlockSpec((S, D_ATT), lambda i, j: (kv_block(i, j), 4))],
        out_specs=pl.BlockSpec((Q, D_ATT), lambda i, j: (i, 0)),
        out_shape=jax.ShapeDtypeStruct((L, D_ATT), BF16),
        scratch_shapes=[pltpu.VMEM((L, Q), jnp.int32),
                        pltpu.VMEM((LANES, Q), F32),
                        pltpu.VMEM((IDX_HEADS, Q, LANES), BF16),
                        pltpu.VMEM((8, Q), jnp.int32),
                        pltpu.VMEM((ATT_HEADS, Q, LANES), F32),
                        pltpu.VMEM((ATT_HEADS, Q, LANES), F32),
                        pltpu.VMEM((Q, D_ATT), F32)],
        compiler_params=_params(("parallel", "arbitrary")),
        name="dsa_attention",
    )(qr, iqr, iws, ik2, kr, zb)


def _head_sum(x, e):
    cols = []
    for p in range(x.shape[1] // LANES):
        cols.append(_dot_exact_rhs(x[:, p * LANES:(p + 1) * LANES], e))
    return jnp.concatenate(cols, axis=1)


def _rwkv_prep_kernel(*refs, T, has_vres):
    if has_vres:
        (z_ref, halo_ref, mu_ref, w0_ref, w2_ref, a0_ref, a2_ref, g2_ref, kk_ref, ka_ref, rk_ref,
         vf_ref, vup_ref, vb_ref,
         rt_ref, at_ref, bt_ref, kt_ref, v_ref, bonus_ref, g_ref, pc_ref) = refs
    else:
        (z_ref, halo_ref, mu_ref, w0_ref, w2_ref, a0_ref, a2_ref, g2_ref, kk_ref, ka_ref, rk_ref,
         rt_ref, at_ref, bt_ref, kt_ref, v_ref, bonus_ref, g_ref, pc_ref) = refs
    i = pl.program_id(0)
    D = D_RWKV

    def shifted(lo, hi):
        z = z_ref[:, lo:hi]
        first = jnp.where(i > 0, halo_ref[7:8, lo:hi], 0.0)
        rows = lax.broadcasted_iota(jnp.int32, z.shape, 0)
        prev = jnp.where(rows == 0, first, pltpu.roll(z, 1, 0))
        return z + (prev - z) * mu_ref[:, lo:hi]

    r = shifted(0, D)
    kraw = shifted(D, 2 * D)
    v = shifted(2 * D, 3 * D)
    xw = shifted(ZR_XW, ZR_XW + LANES)
    xa = shifted(*ZR_XA_WIN)
    xg = shifted(*ZR_XG_WIN)
    if has_vres:
        xv = shifted(*ZR_XV_WIN)
        mix = _sigmoid(vb_ref[...] + _dot(xv.astype(BF16), vup_ref[...]))
        v = v + (vf_ref[...] - v) * mix
    v_ref[...] = v

    w_in = w0_ref[...] + _dot(jnp.tanh(xw).astype(BF16), w2_ref[...])
    w_log = -(jnp.maximum(-w_in, 0.0) + jnp.log(1.0 + jnp.exp(-jnp.abs(w_in)))) - 0.5
    logw = -jnp.exp(w_log)
    a_lr = _sigmoid(a0_ref[...] + _dot(xa.astype(BF16), a2_ref[...]))
    g_ref[...] = _dot(_sigmoid(xg).astype(BF16), g2_ref[...])

    li = lax.broadcasted_iota(jnp.int32, (LANES, LANES), 0) // RWKV_HEAD
    lj = lax.broadcasted_iota(jnp.int32, (LANES, LANES), 1) // RWKV_HEAD
    e_head = jnp.where(li == lj, 1.0, 0.0).astype(BF16)
    kk = kraw * kk_ref[...]
    norm = jnp.sqrt(_head_sum(kk * kk, e_head))
    kk = kk / jnp.maximum(norm, 1e-12)
    k = kraw * (1.0 + (a_lr - 1.0) * ka_ref[...])
    bonus_ref[...] = _head_sum(r * k * rk_ref[...], e_head) * v

    ti = lax.broadcasted_iota(jnp.int32, (T, T), 0)
    tj = lax.broadcasted_iota(jnp.int32, (T, T), 1)
    tri = jnp.where((ti // CHUNK == tj // CHUNK) & (tj <= ti), 1.0, 0.0).astype(BF16)
    cum = _dot_exact_lhs(tri, logw)
    p_in = jnp.exp(cum)
    p_out = jnp.exp(-cum)
    rt_ref[...] = r * p_in
    at_ref[...] = -kk * jnp.exp(cum - logw)
    bt_ref[...] = kk * a_lr * p_out
    kt_ref[...] = k * p_out
    for c in range(T // CHUNK):
        pc_ref[c] = p_in[(c + 1) * CHUNK - 1:(c + 1) * CHUNK, :]


def _rwkv_prep(zr, mu, w0, w2, a0, a2, g2, k_k, k_a, r_k, vres):
    L = zr.shape[0]
    T = min(128, L)
    D = D_RWKV
    row = lambda i: (i, 0)
    const = lambda i: (0, 0)
    vec = lambda a: a.reshape(1, -1)
    has_vres = vres is not None
    ins = [zr, zr, vec(mu), vec(w0), w2, vec(a0), a2, g2, vec(k_k), vec(k_a), vec(r_k)]
    in_specs = [pl.BlockSpec((T, ZR_WIDTH), row),
                pl.BlockSpec((8, ZR_WIDTH), lambda i: (jnp.maximum(i * (T // 8) - 1, 0), 0)),
                pl.BlockSpec((1, ZR_WIDTH), const),
                pl.BlockSpec((1, D), const), pl.BlockSpec(w2.shape, const),
                pl.BlockSpec((1, D), const), pl.BlockSpec(a2.shape, const),
                pl.BlockSpec(g2.shape, const),
                pl.BlockSpec((1, D), const), pl.BlockSpec((1, D), const), pl.BlockSpec((1, D), const)]
    if has_vres:
        v_first, v_up, v_bias = vres
        ins += [v_first, v_up, vec(v_bias)]
        in_specs += [pl.BlockSpec((T, D), row), pl.BlockSpec((LANES, D), const),
                     pl.BlockSpec((1, D), const)]
    big = jax.ShapeDtypeStruct((L, D), F32)
    return pl.pallas_call(
        functools.partial(_rwkv_prep_kernel, T=T, has_vres=has_vres),
        grid=(L // T,),
        in_specs=in_specs,
        out_specs=[pl.BlockSpec((T, D), row)] * 7
                  + [pl.BlockSpec((T // CHUNK, 1, D), lambda i: (i, 0, 0))],
        out_shape=[big] * 7 + [jax.ShapeDtypeStruct((L // CHUNK, 1, D), F32)],
        compiler_params=_params(("parallel",)),
        name="rwkv_prep",
    )(*ins)


def _stack2(x, lo):
    zero = jnp.zeros_like(x)
    return jnp.concatenate([jnp.where(lo, x, zero), jnp.where(lo, zero, x)], axis=0)


def _scan_a_kernel(rt_ref, at_ref, bt_ref, kt_ref, v_ref, pc_ref, rh_ref, y0_ref, g_ref, h_ref,
                   *, CB, UNROLL):
    C = CHUNK
    lane = lax.broadcasted_iota(jnp.int32, (C, LANES), 1)
    lo = lane < RWKV_HEAD
    spos = lane % RWKV_HEAD
    tpos = lax.broadcasted_iota(jnp.int32, (C, LANES), 0)
    strict = spos < tpos
    incl = spos <= tpos
    eye_c = jnp.where(spos == tpos, 1.0, 0.0)
    ri = lax.broadcasted_iota(jnp.int32, (LANES, LANES), 0)
    ci = lax.broadcasted_iota(jnp.int32, (LANES, LANES), 1)
    same_head = (ri // RWKV_HEAD) == (ci // RWKV_HEAD)
    diag = ri == ci

    def stk(x):
        return _stack2(x, lo).astype(BF16)

    nt = lambda a, b: lax.dot_general(a, b, NT_DIMS, preferred_element_type=F32)
    tn = lambda a, b: lax.dot_general(a, b, TN_DIMS, preferred_element_type=F32)

    def each(fn, *cols):
        return [fn(*args) for args in zip(*cols)]

    def body(cg, carry):
        cs = [cg * UNROLL + u for u in range(UNROLL)]
        rows = [pl.ds(pl.multiple_of(c * C, C), C) for c in cs]
        rt = [rt_ref[r, :] for r in rows]
        at = [at_ref[r, :] for r in rows]
        bt = [bt_ref[r, :] for r in rows]
        kt = [kt_ref[r, :] for r in rows]
        v = [v_ref[r, :] for r in rows]
        pc = [pc_ref[c] for c in cs]
        at_b = each(lambda x: x.astype(BF16), at)
        rt_b = each(lambda x: x.astype(BF16), rt)
        bs = each(stk, bt)
        ks = each(stk, kt)
        vs = each(stk, v)
        m_ab = each(lambda a, b: jnp.where(strict, nt(a, b), 0.0), at_b, bs)
        m_ak = each(lambda a, b: jnp.where(strict, nt(a, b), 0.0), at_b, ks)
        m_rb = each(lambda a, b: jnp.where(incl, nt(a, b), 0.0), rt_b, bs)
        m_rk = each(lambda a, b: jnp.where(incl, nt(a, b), 0.0), rt_b, ks)
        mv = each(lambda m, x: _dot(m.astype(BF16), x), m_ak, vs)
        pw = m_ab
        inv = each(lambda m: eye_c + m, pw)
        for _ in range(5):
            pw = each(lambda x: _dot(x.astype(BF16), stk(x)), pw)
            inv = each(lambda t, x: t + _dot(t.astype(BF16), stk(x)), inv, pw)
        inv_b = each(lambda x: x.astype(BF16), inv)
        w = each(lambda t, x: _dot(t, stk(x)), inv_b, at)
        u0 = each(lambda t, x: _dot(t, stk(x)), inv_b, mv)
        m_rb_b = each(lambda x: x.astype(BF16), m_rb)
        rh = each(lambda r, m, x: r + _dot(m, stk(x)), rt, m_rb_b, w)
        y0 = each(lambda m, x, m2, x2: _dot(m, stk(x)) + _dot(m2.astype(BF16), x2),
                  m_rb_b, u0, m_rk, vs)
        btp = each(lambda x, p: (x * p).astype(BF16), bt, pc)
        ktp = each(lambda x, p: (x * p).astype(BF16), kt, pc)
        gm = each(lambda b, x, p: jnp.where(diag, jnp.broadcast_to(p, (LANES, LANES)), 0.0)
                  + jnp.where(same_head, tn(b, x.astype(BF16)), 0.0), btp, w, pc)
        hm = each(lambda b, x, k, y: jnp.where(same_head, tn(b, x.astype(BF16))
                                               + tn(k, y.astype(BF16)), 0.0), btp, u0, ktp, v)
        for u in range(UNROLL):
            rh_ref[rows[u], :] = rh[u]
            y0_ref[rows[u], :] = y0[u]
            g_ref[cs[u], 0] = gm[u]
            h_ref[cs[u], 0] = hm[u]
        return carry

    lax.fori_loop(0, CB // UNROLL, body, 0)


def _scan_a(rt, at, bt, kt, v, pc):
    L, D = rt.shape
    NP = D // LANES
    NC = L // CHUNK
    CB = min(8, NC)
    blk = pl.BlockSpec((CB * CHUNK, LANES), lambda ci, p: (ci, p))
    gh = pl.BlockSpec((CB, 1, LANES, LANES), lambda ci, p: (ci, p, 0, 0))
    return pl.pallas_call(
        functools.partial(_scan_a_kernel, CB=CB, UNROLL=min(8, CB)),
        grid=(NC // CB, NP),
        in_specs=[blk] * 5 + [pl.BlockSpec((CB, 1, LANES), lambda ci, p: (ci, 0, p))],
        out_specs=[blk, blk, gh, gh],
        out_shape=[jax.ShapeDtypeStruct((L, D), F32), jax.ShapeDtypeStruct((L, D), F32),
                   jax.ShapeDtypeStruct((NC, NP, LANES, LANES), F32),
                   jax.ShapeDtypeStruct((NC, NP, LANES, LANES), F32)],
        compiler_params=_params(("parallel", "parallel")),
        name="rwkv_chunk_local",
    )(rt, at, bt, kt, v, pc)


def _scan_b_kernel(rh_ref, y0_ref, g_ref, h_ref, bonus_ref, gate_ref, lg_ref, lb_ref, o_ref,
                   st_ref, *, CB, PP):
    ci = pl.program_id(1)

    @pl.when(ci == 0)
    def _():
        st_ref[...] = jnp.zeros(st_ref.shape, F32)

    li = lax.broadcasted_iota(jnp.int32, (LANES, LANES), 0) // RWKV_HEAD
    lj = lax.broadcasted_iota(jnp.int32, (LANES, LANES), 1) // RWKV_HEAD
    e_mean = jnp.where(li == lj, 1.0 / RWKV_HEAD, 0.0).astype(BF16)

    def body(c, carry):
        rows = pl.ds(pl.multiple_of(c * CHUNK, CHUNK), CHUNK)
        sls = [slice(p * LANES, (p + 1) * LANES) for p in range(PP)]

        def each(fn, *cols):
            return [fn(*args) for args in zip(*cols)]

        st = [st_ref[p] for p in range(PP)]
        y = each(lambda sl, s: _dot3(rh_ref[rows, sl], s) + y0_ref[rows, sl], sls, st)
        st_new = each(lambda p, s: _dot3(g_ref[c, p], s) + h_ref[c, p], list(range(PP)), st)
        for p in range(PP):
            st_ref[p] = st_new[p]
        mean = each(lambda x: _dot_exact_rhs(x, e_mean), y)
        d = each(lambda x, m: x - m, y, mean)
        var = each(lambda x: _dot_exact_rhs(x * x, e_mean), d)
        for p in range(PP):
            sl = sls[p]
            yn = d[p] * lax.rsqrt(var[p] + RWKV_GN_EPS) * lg_ref[:, sl] + lb_ref[:, sl]
            o_ref[rows, sl] = ((yn + bonus_ref[rows, sl]) * gate_ref[rows, sl]).astype(o_ref.dtype)
        return carry

    lax.fori_loop(0, CB, body, 0)


def _scan_b(rh, y0, g, h, bonus, gate, ln_g, ln_b):
    L, D = rh.shape
    NP = D // LANES
    NC = L // CHUNK
    CB = min(8, NC)
    PP = 8
    blk = pl.BlockSpec((CB * CHUNK, PP * LANES), lambda pg, ci: (ci, pg))
    gh = pl.BlockSpec((CB, PP, LANES, LANES), lambda pg, ci: (ci, pg, 0, 0))
    vec = pl.BlockSpec((1, PP * LANES), lambda pg, ci: (0, pg))
    return pl.pallas_call(
        functools.partial(_scan_b_kernel, CB=CB, PP=PP),
        grid=(NP // PP, NC // CB),
        in_specs=[blk, blk, gh, gh, blk, blk, vec, vec],
        out_specs=blk,
        out_shape=jax.ShapeDtypeStruct((L, D), BF16),
        scratch_shapes=[pltpu.VMEM((PP, LANES, LANES), F32)],
        compiler_params=_params(("parallel", "arbitrary")),
        name="rwkv_state_scan",
    )(rh, y0, g, h, bonus, gate, ln_g.reshape(1, -1), ln_b.reshape(1, -1))


def _ct_kernel(w_ref, o_ref):
    o_ref[...] = w_ref[0].T.astype(BF16)


def _ct_patch_kernel(w_ref, patch_ref, o_ref, *, first_patch_row):
    w = w_ref[0]
    rows = lax.broadcasted_iota(jnp.int32, w.shape, 0)
    last = pl.program_id(0) == pl.num_programs(0) - 1
    w = jnp.where(last & (rows >= first_patch_row), patch_ref[...], w)
    o_ref[...] = w.T.astype(BF16)


def _cast_transpose(w_t, l, n_cols, tn, row_of_tile, name, patch=None, first_patch_row=0):
    K = w_t.shape[2]
    in_specs = [pl.BlockSpec((pl.Element(1), pl.Element(tn), pl.Element(K)),
                             lambda j: (l, pl.multiple_of(row_of_tile(j), 16), 0))]
    args = [w_t]
    body = _ct_kernel
    if patch is not None:
        in_specs.append(pl.BlockSpec((tn, K), lambda j: (0, 0)))
        args.append(patch)
        body = functools.partial(_ct_patch_kernel, first_patch_row=first_patch_row)
    return pl.pallas_call(
        body,
        grid=(n_cols // tn,),
        in_specs=in_specs,
        out_specs=pl.BlockSpec((K, tn), lambda j: (0, j)),
        out_shape=jax.ShapeDtypeStruct((K, n_cols), BF16),
        compiler_params=_params(("parallel",)),
        name=name,
    )(*args)


def _pack_in_proj(w_t, l, vdown):
    tn = 512
    n_plain = IN_IDX // tn
    w_b = _cast_transpose(
        w_t, l, ZB_WIDTH, tn,
        lambda j: jnp.where(j < n_plain, j * tn, IN_GATE + (j - n_plain) * tn), "pack_w_b")
    n_own = ZR_XV - (ZR_WIDTH - tn)
    patch = jnp.pad(vdown.T, ((n_own, 0), (0, 0)))
    w_r = _cast_transpose(w_t, l, ZR_WIDTH, tn, lambda j: IN_RWKV + j * tn, "pack_w_r",
                          patch=patch, first_patch_row=n_own)
    w_i = _cast_transpose(w_t, l, ZI_WIDTH, 256, lambda j: IN_IDX + j * 256, "pack_w_i")
    return w_b, w_r, w_i


def _rows_at(w, start, height):
    return jnp.pad(w, ((start, height - start - w.shape[0]), (0, 0)))


def kernel(x, w_in, norm_mix, dw_weight, dw_bias, conv_ln_g, conv_ln_b, w_conv_out, w_att_out, rwkv_mu, rwkv_w0, rwkv_w2, rwkv_a0, rwkv_a2, rwkv_g2, rwkv_k_k, rwkv_k_a, rwkv_r_k, rwkv_ln_g, rwkv_ln_b, vres_down, vres_mu, vres_up, vres_bias, w_rwkv_out, w_out, norm_ffn, w_ffn_gate, w_ffn_up, w_ffn_down, norm_final):
    B, L, D = x.shape
    assert B == 1 and D == D_MODEL and L % 256 == 0
    depth = w_in.shape[0]
    xs = x.reshape(L, D)
    assert w_in.shape[1:] == (D_MODEL, D_IN)
    w_t = jnp.swapaxes(w_in, 1, 2)
    wc_b, wa_b, wr_b, wo_b = (w.astype(BF16) for w in (w_conv_out, w_att_out, w_rwkv_out, w_out))
    wg_b, wu_b, wd_b = (w.astype(BF16) for w in (w_ffn_gate, w_ffn_up, w_ffn_down))
    v_first = None
    for l in range(depth):
        vdown = vres_down[l - 1] if l > 0 else jnp.zeros((D, LORA_MV), F32)
        w_b, w_r, w_i = _pack_in_proj(w_t, l, vdown)
        mu_p = jnp.concatenate([rwkv_mu[l], vres_mu[l - 1] if l > 0 else jnp.zeros((LORA_MV,), F32)])
        w2_p = _rows_at(rwkv_w2[l], 0, LANES).astype(BF16)
        a2_p = _rows_at(rwkv_a2[l], ZR_XA - ZR_XA_WIN[0], ZR_XA_WIN[1] - ZR_XA_WIN[0]).astype(BF16)
        g2_p = _rows_at(rwkv_g2[l], ZR_XG - ZR_XG_WIN[0], ZR_XG_WIN[1] - ZR_XG_WIN[0]).astype(BF16)

        h = _rms_norm(xs, norm_mix[l], BF16)
        zb = _matmul(h, w_b, BF16, 1024, 512, "in_proj_bf16")
        zr = _matmul(h, w_r, F32, 1024, 512, "in_proj_rwkv")
        zi = _matmul(h, w_i, F32, 1024, 256, "in_proj_index")

        a_mix = _conformer_conv(zb, dw_weight[l], dw_bias[l], conv_ln_g[l], conv_ln_b[l])

        qr, kr, iqr, ik2, iws = _att_prep(zb, zi)
        b_mix = _dsa_attention(qr, iqr, iws, ik2, kr, zb)

        vres = None
        if l > 0:
            vup_p = _rows_at(vres_up[l - 1], ZR_XV - ZR_XV_WIN[0], LANES).astype(BF16)
            vres = (v_first, vup_p, vres_bias[l - 1])
        rt, at, bt, kt, v_rwkv, bonus, gate, pc = _rwkv_prep(
            zr, mu_p, rwkv_w0[l], w2_p, rwkv_a0[l], a2_p, g2_p,
            rwkv_k_k[l], rwkv_k_a[l], rwkv_r_k[l].reshape(-1), vres)
        if l == 0:
            v_first = v_rwkv
        rh, y0, g_mat, h_mat = _scan_a(rt, at, bt, kt, v_rwkv, pc)
        c_mix = _scan_b(rh, y0, g_mat, h_mat, bonus, gate, rwkv_ln_g[l], rwkv_ln_b[l])

        merged = _merge(a_mix, b_mix, c_mix, wc_b, wa_b, wr_b, l, zb, 1024, 512)
        xs = _matmul_residual(merged, wo_b, l, xs, 1024, 512, "out_proj")

        h2 = _rms_norm(xs, norm_ffn[l], BF16)
        act = _ffn1(h2, wg_b, wu_b, l, 1024, 256)
        xs = _matmul_residual(act, wd_b, l, xs, 512, 256, "ffn_down")
    return _rms_norm(xs, norm_final, F32).reshape(B, L, D)
```

```python
import functools

import jax
import jax.numpy as jnp
import numpy as np
from jax import lax
from jax.experimental import pallas as pl
from jax.experimental.pallas import tpu as pltpu

F32 = jnp.float32
BF16 = jnp.bfloat16

D_MODEL = 4096
RMS_EPS = 1e-6
LN_EPS = 1e-5
D_CONV = D_MODEL // 4
CONV_WIDTH = 31
ATT_HEADS = 8
ATT_HEAD_DIM = 128
D_ATT = ATT_HEADS * ATT_HEAD_DIM
ATT_SCALE = ATT_HEAD_DIM ** -0.5
LOG2_E = 1.4426950408889634
IDX_HEADS = 16
IDX_HEAD_DIM = 64
IDX_SCALE = IDX_HEAD_DIM ** -0.5
IDX_W_SCALE = IDX_HEADS ** -0.5
TOPK_MAX = 256
ROPE_THETA = 500000.0
ROPE_FRACTION = 4
D_RWKV = D_MODEL // 2
RWKV_HEAD = 64
LORA_DECAY = 96
LORA_AAA = 96
LORA_MV = 64
LORA_GATE = 256
RWKV_GN_EPS = 64e-5
D_FF = ((8 * D_MODEL + 3 * 256 - 1) // (3 * 256)) * 256

LANES = 128
CHUNK = 64
IN_ATT = 2 * D_CONV
IN_IDX = IN_ATT + 3 * D_ATT
IN_RWKV = IN_IDX + IDX_HEADS * IDX_HEAD_DIM + IDX_HEAD_DIM + IDX_HEADS
IN_GATE = IN_RWKV + 3 * D_RWKV + LORA_DECAY + LORA_AAA + LORA_GATE
D_IN = IN_GATE + 3 * D_MODEL
ZR_XW = 3 * D_RWKV
ZR_XA = ZR_XW + LORA_DECAY
ZR_XG = ZR_XA + LORA_AAA
ZR_XV = ZR_XG + LORA_GATE
ZR_WIDTH = ZR_XV + LORA_MV
ZR_XA_WIN = (ZR_XW, ZR_XW + 2 * LANES)
ZR_XG_WIN = (ZR_XW + LANES, ZR_WIDTH)
ZR_XV_WIN = (ZR_WIDTH - LANES, ZR_WIDTH)
ZI_IK = IDX_HEADS * IDX_HEAD_DIM
ZI_IW_LANE = IDX_HEAD_DIM
ZI_WIDTH = ZI_IK + 2 * LANES
ZB_GATE = 2 * D_CONV + 3 * D_ATT
ZB_WIDTH = ZB_GATE + 3 * D_MODEL
VMEM_LIMIT = 56 * 1024 * 1024
INT_MIN = -2147483648
NEG_BIG = -1e30

NT_DIMS = (((1,), (1,)), ((), ()))
TN_DIMS = (((0,), (0,)), ((), ()))


def _params(sem, vmem=VMEM_LIMIT):
    return pltpu.CompilerParams(dimension_semantics=sem, vmem_limit_bytes=vmem)


def _dot(a, b):
    return jnp.dot(a, b, preferred_element_type=F32)


def _split2(x):
    hi = x.astype(BF16)
    lo = (x - hi.astype(F32)).astype(BF16)
    return hi, lo


def _dot_exact_rhs(x, e):
    h1 = x.astype(BF16)
    r1 = x - h1.astype(F32)
    h2 = r1.astype(BF16)
    h3 = (r1 - h2.astype(F32)).astype(BF16)
    return _dot(h1, e) + _dot(h2, e) + _dot(h3, e)


def _dot_exact_lhs(e, x):
    h1 = x.astype(BF16)
    r1 = x - h1.astype(F32)
    h2 = r1.astype(BF16)
    h3 = (r1 - h2.astype(F32)).astype(BF16)
    return _dot(e, h1) + _dot(e, h2) + _dot(e, h3)


def _dot3(a, b):
    ah, al = _split2(a)
    bh, bl = _split2(b)
    return _dot(ah, bh) + _dot(ah, bl) + _dot(al, bh)


def _sigmoid(x):
    return 1.0 / (1.0 + jnp.exp(-x))


def _rms_kernel(x_ref, g_ref, o_ref):
    x = x_ref[...]
    y = x * lax.rsqrt(jnp.mean(x * x, axis=-1, keepdims=True) + RMS_EPS)
    o_ref[...] = (y * g_ref[...]).astype(o_ref.dtype)


def _rms_norm(x, g, out_dtype):
    L, D = x.shape
    tr = min(256, L)
    return pl.pallas_call(
        _rms_kernel,
        grid=(L // tr,),
        in_specs=[pl.BlockSpec((tr, D), lambda i: (i, 0)),
                  pl.BlockSpec((1, D), lambda i: (0, 0))],
        out_specs=pl.BlockSpec((tr, D), lambda i: (i, 0)),
        out_shape=jax.ShapeDtypeStruct((L, D), out_dtype),
        compiler_params=_params(("parallel",)),
        name="rms_norm",
    )(x, g.reshape(1, D))


def _mm_kernel(a_ref, b_ref, o_ref):
    o_ref[...] = _dot(a_ref[...], b_ref[...]).astype(o_ref.dtype)


def _matmul(a, b, out_dtype, tm, tn, name):
    M, K = a.shape
    N = b.shape[1]
    tm = min(tm, M)
    return pl.pallas_call(
        _mm_kernel,
        grid=(M // tm, N // tn),
        in_specs=[pl.BlockSpec((tm, K), lambda i, j: (i, 0)),
                  pl.BlockSpec((K, tn), lambda i, j: (0, j))],
        out_specs=pl.BlockSpec((tm, tn), lambda i, j: (i, j)),
        out_shape=jax.ShapeDtypeStruct((M, N), out_dtype),
        compiler_params=_params(("parallel", "parallel")),
        name=name,
    )(a, b)


def _mm_res_kernel(a_ref, b_ref, x_ref, o_ref):
    o_ref[...] = x_ref[...] + _dot(a_ref[...], b_ref[...])


def _layer_weight_spec(w, l, tn):
    return pl.BlockSpec((None, w.shape[1], tn), lambda i, j: (l, 0, j))


def _matmul_residual(a, b, l, x, tm, tn, name):
    M, K = a.shape
    N = b.shape[2]
    tm = min(tm, M)
    return pl.pallas_call(
        _mm_res_kernel,
        grid=(M // tm, N // tn),
        in_specs=[pl.BlockSpec((tm, K), lambda i, j: (i, 0)),
                  _layer_weight_spec(b, l, tn),
                  pl.BlockSpec((tm, tn), lambda i, j: (i, j))],
        out_specs=pl.BlockSpec((tm, tn), lambda i, j: (i, j)),
        out_shape=jax.ShapeDtypeStruct((M, N), F32),
        compiler_params=_params(("parallel", "parallel")),
        name=name,
    )(a, b, x)


def _ffn1_kernel(h_ref, wg_ref, wu_ref, o_ref):
    h = h_ref[...]
    g = _dot(h, wg_ref[...])
    u = _dot(h, wu_ref[...])
    o_ref[...] = (g * _sigmoid(g) * u).astype(o_ref.dtype)


def _ffn1(h, wg, wu, l, tm, tn):
    M, K = h.shape
    N = wg.shape[2]
    tm = min(tm, M)
    return pl.pallas_call(
        _ffn1_kernel,
        grid=(M // tm, N // tn),
        in_specs=[pl.BlockSpec((tm, K), lambda i, j: (i, 0)),
                  _layer_weight_spec(wg, l, tn),
                  _layer_weight_spec(wu, l, tn)],
        out_specs=pl.BlockSpec((tm, tn), lambda i, j: (i, j)),
        out_shape=jax.ShapeDtypeStruct((M, N), BF16),
        compiler_params=_params(("parallel", "parallel")),
        name="ffn_gate_up",
    )(h, wg, wu)


def _merge_kernel(a_ref, b_ref, c_ref, wa_ref, wb_ref, wc_ref, g0_ref, g1_ref, g2_ref, o_ref):
    ya = _dot(a_ref[...], wa_ref[...])
    yb = _dot(b_ref[...], wb_ref[...])
    yc = _dot(c_ref[...], wc_ref[...])
    m = (_sigmoid(g0_ref[...].astype(F32)) * ya + _sigmoid(g1_ref[...].astype(F32)) * yb
         + _sigmoid(g2_ref[...].astype(F32)) * yc)
    o_ref[...] = m.astype(o_ref.dtype)


def _merge(a, b, c, wa, wb, wc, l, zb, tm, tn):
    M = a.shape[0]
    tm = min(tm, M)
    g_off = ZB_GATE // tn
    g_stride = D_MODEL // tn
    return pl.pallas_call(
        _merge_kernel,
        grid=(M // tm, D_MODEL // tn),
        in_specs=[pl.BlockSpec((tm, D_CONV), lambda i, j: (i, 0)),
                  pl.BlockSpec((tm, D_ATT), lambda i, j: (i, 0)),
                  pl.BlockSpec((tm, D_RWKV), lambda i, j: (i, 0)),
                  _layer_weight_spec(wa, l, tn),
                  _layer_weight_spec(wb, l, tn),
                  _layer_weight_spec(wc, l, tn),
                  pl.BlockSpec((tm, tn), lambda i, j: (i, g_off + j)),
                  pl.BlockSpec((tm, tn), lambda i, j: (i, g_off + g_stride + j)),
                  pl.BlockSpec((tm, tn), lambda i, j: (i, g_off + 2 * g_stride + j))],
        out_specs=pl.BlockSpec((tm, tn), lambda i, j: (i, j)),
        out_shape=jax.ShapeDtypeStruct((M, D_MODEL), BF16),
        compiler_params=_params(("parallel", "parallel")),
        name="gated_merge",
    )(a, b, c, wa, wb, wc, zb, zb, zb)


def _conv_kernel(val_ref, gate_ref, hval_ref, hgate_ref, w_ref, b_ref, lg_ref, lb_ref, o_ref,
                 c_ref, *, T):
    i = pl.program_id(0)
    halo = 32
    hv = hval_ref[...].astype(F32)
    hg = hgate_ref[...].astype(F32)
    hc = hv * _sigmoid(hg)
    c_ref[0:halo, :] = jnp.where(i > 0, hc, 0.0)
    v = val_ref[...].astype(F32)
    g = gate_ref[...].astype(F32)
    c_ref[halo:halo + T, :] = v * _sigmoid(g)
    acc = jnp.zeros((T, D_CONV), F32) + b_ref[...]
    for j in range(CONV_WIDTH):
        off = halo - (CONV_WIDTH - 1) + j
        acc = acc + c_ref[off:off + T, :] * w_ref[j:j + 1, :]
    m = jnp.mean(acc, axis=-1, keepdims=True)
    d = acc - m
    var = jnp.mean(d * d, axis=-1, keepdims=True)
    y = d * lax.rsqrt(var + LN_EPS) * lg_ref[...] + lb_ref[...]
    o_ref[...] = (y * _sigmoid(y)).astype(o_ref.dtype)


def _conformer_conv(zb, dw_w, dw_b, ln_g, ln_b):
    L = zb.shape[0]
    T = min(256, L)
    hb = T // 32
    w_pad = jnp.zeros((32, D_CONV), F32).at[:CONV_WIDTH].set(dw_w)
    return pl.pallas_call(
        functools.partial(_conv_kernel, T=T),
        grid=(L // T,),
        in_specs=[pl.BlockSpec((T, D_CONV), lambda i: (i, 0)),
                  pl.BlockSpec((T, D_CONV), lambda i: (i, 1)),
                  pl.BlockSpec((32, D_CONV), lambda i: (jnp.maximum(i * hb - 1, 0), 0)),
                  pl.BlockSpec((32, D_CONV), lambda i: (jnp.maximum(i * hb - 1, 0), 1)),
                  pl.BlockSpec((32, D_CONV), lambda i: (0, 0)),
                  pl.BlockSpec((1, D_CONV), lambda i: (0, 0)),
                  pl.BlockSpec((1, D_CONV), lambda i: (0, 0)),
                  pl.BlockSpec((1, D_CONV), lambda i: (0, 0))],
        out_specs=pl.BlockSpec((T, D_CONV), lambda i: (i, 0)),
        out_shape=jax.ShapeDtypeStruct((L, D_CONV), BF16),
        scratch_shapes=[pltpu.VMEM((T + 32, D_CONV), F32)],
        compiler_params=_params(("parallel",)),
        name="conformer_conv",
    )(zb, zb, zb, zb, w_pad, dw_b.reshape(1, -1), ln_g.reshape(1, -1), ln_b.reshape(1, -1))


def _rope_tables(pos, inv_freq, head_dim):
    rd = head_dim // ROPE_FRACTION
    half = rd // 2
    ang = pos * inv_freq
    cos = jnp.cos(ang)
    sin = jnp.sin(ang)
    lane = lax.broadcasted_iota(jnp.int32, ang.shape, 1) % head_dim
    c = jnp.where(lane < rd, cos, 1.0)
    s_up = jnp.where(lane < half, -sin, 0.0)
    s_dn = jnp.where((lane >= half) & (lane < rd), sin, 0.0)
    return c, s_up, s_dn


def _rope_apply(x, tabs, half):
    c, s_up, s_dn = tabs
    return (x * c + pltpu.roll(x, LANES - half, 1) * s_up + pltpu.roll(x, half, 1) * s_dn)


def _attprep_kernel(q_ref, k_ref, zi_ref, fq_ref, fi_ref, qo_ref, ko_ref, iqo_ref, iko_ref,
                    iwo_ref, *, T):
    i = pl.program_id(0)
    pos = (i * T + lax.broadcasted_iota(jnp.int32, (T, LANES), 0)).astype(F32)
    tq = _rope_tables(pos, fq_ref[...], ATT_HEAD_DIM)
    ti = _rope_tables(pos, fi_ref[...], IDX_HEAD_DIM)
    hq = ATT_HEAD_DIM // ROPE_FRACTION // 2
    hi = IDX_HEAD_DIM // ROPE_FRACTION // 2
    for h in range(ATT_HEADS):
        sl = slice(h * LANES, (h + 1) * LANES)
        qo_ref[:, sl] = (_rope_apply(q_ref[:, sl].astype(F32), tq, hq)
                         * (ATT_SCALE * LOG2_E)).astype(BF16)
        ko_ref[:, sl] = _rope_apply(k_ref[:, sl].astype(F32), tq, hq).astype(BF16)
    for p in range(ZI_IK // LANES):
        sl = slice(p * LANES, (p + 1) * LANES)
        iqo_ref[:, sl] = _rope_apply(zi_ref[:, sl], ti, hi).astype(BF16)
    grp = zi_ref[:, ZI_IK:ZI_IK + LANES]
    ik = _rope_apply(grp, ti, hi)
    lane = lax.broadcasted_iota(jnp.int32, (T, LANES), 1)
    iko_ref[...] = jnp.where(lane < IDX_HEAD_DIM, ik, pltpu.roll(ik, IDX_HEAD_DIM, 1)).astype(BF16)
    iwo_ref[...] = grp * (IDX_W_SCALE * IDX_SCALE)


def _lane_inv_freq(head_dim):
    rd = head_dim // ROPE_FRACTION
    half = rd // 2
    inv_freq = jnp.power(ROPE_THETA, -jnp.arange(half, dtype=F32) * (2.0 / rd))
    lane = np.arange(LANES) % head_dim
    idx = np.where(lane < rd, lane % half, 0)
    return jnp.where(jnp.asarray(lane < rd), inv_freq[idx], 0.0).reshape(1, LANES)


def _att_prep(zb, zi):
    L = zb.shape[0]
    T = min(256, L)
    row = lambda i: (i, 0)
    return pl.pallas_call(
        functools.partial(_attprep_kernel, T=T),
        grid=(L // T,),
        in_specs=[pl.BlockSpec((T, D_ATT), lambda i: (i, 2)),
                  pl.BlockSpec((T, D_ATT), lambda i: (i, 3)),
                  pl.BlockSpec((T, ZI_WIDTH), row),
                  pl.BlockSpec((1, LANES), lambda i: (0, 0)),
                  pl.BlockSpec((1, LANES), lambda i: (0, 0))],
        out_specs=[pl.BlockSpec((T, D_ATT), row), pl.BlockSpec((T, D_ATT), row),
                   pl.BlockSpec((T, ZI_IK), row), pl.BlockSpec((T, LANES), row),
                   pl.BlockSpec((T, LANES), row)],
        out_shape=[jax.ShapeDtypeStruct((L, D_ATT), BF16), jax.ShapeDtypeStruct((L, D_ATT), BF16),
                   jax.ShapeDtypeStruct((L, ZI_IK), BF16), jax.ShapeDtypeStruct((L, LANES), BF16),
                   jax.ShapeDtypeStruct((L, LANES), F32)],
        compiler_params=_params(("parallel",)),
        name="att_prep",
    )(zb, zb, zi, _lane_inv_freq(ATT_HEAD_DIM), _lane_inv_freq(IDX_HEAD_DIM))


def _dsa_kernel(q_ref, iq_ref, iw_ref, ik_ref, k_ref, v_ref, o_ref,
                keyt_ref, wt_ref, iqm_ref, thr_ref, m_ref, l_ref, acc_ref, *, Q, S, SLAB, topk):
    i = pl.program_id(0)
    j = pl.program_id(1)
    q_end = (i + 1) * Q

    @pl.when(j == 0)
    def _select():
        lane = lax.broadcasted_iota(jnp.int32, (Q, LANES), 1)
        lo = lane < IDX_HEAD_DIM
        zero = jnp.zeros((Q, LANES), BF16)
        for p in range(IDX_HEADS // 2):
            x = iq_ref[:, p * LANES:(p + 1) * LANES]
            iqm_ref[2 * p] = jnp.where(lo, x, zero)
            iqm_ref[2 * p + 1] = jnp.where(lo, zero, x)
        wt_ref[...] = iw_ref[...].T
        qpos = i * Q + lax.broadcasted_iota(jnp.int32, (S, Q), 1)

        def score_chunk(c, carry):
            off = pl.multiple_of(c * S, S)
            ikc = ik_ref[pl.ds(off, S), :]
            sc = jnp.zeros((S, Q), F32)
            for h in range(IDX_HEADS):
                lg = lax.dot_general(ikc, iqm_ref[h], NT_DIMS, preferred_element_type=F32)
                sc = sc + jnp.maximum(lg, 0.0) * wt_ref[ZI_IW_LANE + h:ZI_IW_LANE + h + 1, :]
            sc = jnp.where(sc == 0.0, 0.0, sc)
            bits = pltpu.bitcast(sc, jnp.int32)
            key = bits ^ ((bits >> 31) & 0x7FFFFFFF)
            kpos = off + lax.broadcasted_iota(jnp.int32, (S, Q), 0)
            keyt_ref[pl.ds(off, S), :] = jnp.where(kpos <= qpos, key, INT_MIN)
            return carry

        nchunk = (q_end + S - 1) // S
        lax.fori_loop(0, nchunk, score_chunk, 0)

        def bit_step(b, prefix):
            cand_u = prefix | jnp.left_shift(jnp.int32(1), 31 - b)
            cand = jnp.broadcast_to(cand_u ^ INT_MIN, (SLAB, Q))

            def count(c, cnt):
                off = pl.multiple_of(c * S, S)
                for u in range(S // SLAB):
                    kc = keyt_ref[pl.ds(off + u * SLAB, SLAB), :]
                    cnt = cnt + jnp.where(kc >= cand, 1.0, 0.0)
                return cnt

            cnt = lax.fori_loop(0, nchunk, count, jnp.zeros((SLAB, Q), F32))
            tot = jnp.sum(cnt, axis=0, keepdims=True)
            return jnp.where(tot >= float(topk), cand_u, prefix)

        prefix = lax.fori_loop(0, 32, bit_step, jnp.zeros((1, Q), jnp.int32))
        thr = jnp.maximum(prefix ^ INT_MIN, INT_MIN + 1)
        thr_ref[...] = jnp.broadcast_to(thr, thr_ref.shape)
        m_ref[...] = jnp.full(m_ref.shape, NEG_BIG, F32)
        l_ref[...] = jnp.zeros(l_ref.shape, F32)
        acc_ref[...] = jnp.zeros(acc_ref.shape, F32)

    @pl.when(j * S < q_end)
    def _attend():
        keys = keyt_ref[pl.ds(pl.multiple_of(j * S, S), S), :]
        bias = jnp.where(keys >= thr_ref[0:1, :], 0.0, NEG_BIG).T
        for h in range(ATT_HEADS):
            sl = slice(h * LANES, (h + 1) * LANES)
            s = lax.dot_general(q_ref[:, sl], k_ref[:, sl], NT_DIMS, preferred_element_type=F32)
            s = s + bias
            m_prev = m_ref[h]
            m_new = jnp.maximum(m_prev, jnp.max(s, axis=1, keepdims=True))
            alpha = jnp.exp2(m_prev - m_new)
            p = jnp.exp2(s - jnp.tile(m_new, (1, S // LANES)))
            l_ref[h] = alpha * l_ref[h] + jnp.sum(p, axis=1, keepdims=True)
            acc_ref[:, sl] = alpha * acc_ref[:, sl] + _dot(p.astype(BF16), v_ref[:, sl])
            m_ref[h] = m_new

    @pl.when(j == pl.num_programs(1) - 1)
    def _finish():
        for h in range(ATT_HEADS):
            sl = slice(h * LANES, (h + 1) * LANES)
            o_ref[:, sl] = (acc_ref[:, sl] / l_ref[h]).astype(o_ref.dtype)


def _dsa_attention(qr, iqr, iws, ik2, kr, zb):
    L = qr.shape[0]
    Q = min(512, L)
    S = min(512, L)
    SLAB = 32
    topk = min(TOPK_MAX, L // 4)
    nk = L // S

    def kv_block(i, j):
        return jnp.minimum(j, ((i + 1) * Q - 1) // S)

    return pl.pallas_call(
        functools.partial(_dsa_kernel, Q=Q, S=S, SLAB=SLAB, topk=topk),
        grid=(L // Q, nk),
        in_specs=[pl.BlockSpec((Q, D_ATT), lambda i, j: (i, 0)),
                  pl.BlockSpec((Q, ZI_IK), lambda i, j: (i, 0)),
                  pl.BlockSpec((Q, LANES), lambda i, j: (i, 0)),
                  pl.BlockSpec((L, LANES), lambda i, j: (0, 0)),
                  pl.BlockSpec((S, D_ATT), lambda i, j: (kv_block(i, j), 0)),
                  pl.BlockSpec((S, D_ATT), lambda i, j: (kv_block(i, j), 4))],
        out_specs=pl.BlockSpec((Q, D_ATT), lambda i, j: (i, 0)),
        out_shape=jax.ShapeDtypeStruct((L, D_ATT), BF16),
        scratch_shapes=[pltpu.VMEM((L, Q), jnp.int32),
                        pltpu.VMEM((LANES, Q), F32),
                        pltpu.VMEM((IDX_HEADS, Q, LANES), BF16),
                        pltpu.VMEM((8, Q), jnp.int32),
                        pltpu.VMEM((ATT_HEADS, Q, LANES), F32),
                        pltpu.VMEM((ATT_HEADS, Q, LANES), F32),
                        pltpu.VMEM((Q, D_ATT), F32)],
        compiler_params=_params(("parallel", "arbitrary")),
        name="dsa_attention",
    )(qr, iqr, iws, ik2, kr, zb)


def _head_sum(x, e):
    cols = []
    for p in range(x.shape[1] // LANES):
        cols.append(_dot_exact_rhs(x[:, p * LANES:(p + 1) * LANES], e))
    return jnp.concatenate(cols, axis=1)


def _rwkv_prep_kernel(*refs, T, has_vres):
    if has_vres:
        (z_ref, halo_ref, mu_ref, w0_ref, w2_ref, a0_ref, a2_ref, g2_ref, kk_ref, ka_ref, rk_ref,
         vf_ref, vup_ref, vb_ref,
         rt_ref, at_ref, bt_ref, kt_ref, v_ref, bonus_ref, g_ref, pc_ref) = refs
    else:
        (z_ref, halo_ref, mu_ref, w0_ref, w2_ref, a0_ref, a2_ref, g2_ref, kk_ref, ka_ref, rk_ref,
         rt_ref, at_ref, bt_ref, kt_ref, v_ref, bonus_ref, g_ref, pc_ref) = refs
    i = pl.program_id(0)
    D = D_RWKV

    def shifted(lo, hi):
        z = z_ref[:, lo:hi]
        first = jnp.where(i > 0, halo_ref[7:8, lo:hi], 0.0)
        rows = lax.broadcasted_iota(jnp.int32, z.shape, 0)
        prev = jnp.where(rows == 0, first, pltpu.roll(z, 1, 0))
        return z + (prev - z) * mu_ref[:, lo:hi]

    r = shifted(0, D)
    kraw = shifted(D, 2 * D)
    v = shifted(2 * D, 3 * D)
    xw = shifted(ZR_XW, ZR_XW + LANES)
    xa = shifted(*ZR_XA_WIN)
    xg = shifted(*ZR_XG_WIN)
    if has_vres:
        xv = shifted(*ZR_XV_WIN)
        mix = _sigmoid(vb_ref[...] + _dot(xv.astype(BF16), vup_ref[...]))
        v = v + (vf_ref[...] - v) * mix
    v_ref[...] = v

    w_in = w0_ref[...] + _dot(jnp.tanh(xw).astype(BF16), w2_ref[...])
    w_log = -(jnp.maximum(-w_in, 0.0) + jnp.log(1.0 + jnp.exp(-jnp.abs(w_in)))) - 0.5
    logw = -jnp.exp(w_log)
    a_lr = _sigmoid(a0_ref[...] + _dot(xa.astype(BF16), a2_ref[...]))
    g_ref[...] = _dot(_sigmoid(xg).astype(BF16), g2_ref[...])

    li = lax.broadcasted_iota(jnp.int32, (LANES, LANES), 0) // RWKV_HEAD
    lj = lax.broadcasted_iota(jnp.int32, (LANES, LANES), 1) // RWKV_HEAD
    e_head = jnp.where(li == lj, 1.0, 0.0).astype(BF16)
    kk = kraw * kk_ref[...]
    norm = jnp.sqrt(_head_sum(kk * kk, e_head))
    kk = kk / jnp.maximum(norm, 1e-12)
    k = kraw * (1.0 + (a_lr - 1.0) * ka_ref[...])
    bonus_ref[...] = _head_sum(r * k * rk_ref[...], e_head) * v

    ti = lax.broadcasted_iota(jnp.int32, (T, T), 0)
    tj = lax.broadcasted_iota(jnp.int32, (T, T), 1)
    tri = jnp.where((ti // CHUNK == tj // CHUNK) & (tj <= ti), 1.0, 0.0).astype(BF16)
    cum = _dot_exact_lhs(tri, logw)
    p_in = jnp.exp(cum)
    p_out = jnp.exp(-cum)
    rt_ref[...] = r * p_in
    at_ref[...] = -kk * jnp.exp(cum - logw)
    bt_ref[...] = kk * a_lr * p_out
    kt_ref[...] = k * p_out
    for c in range(T // CHUNK):
        pc_ref[c] = p_in[(c + 1) * CHUNK - 1:(c + 1) * CHUNK, :]


def _rwkv_prep(zr, mu, w0, w2, a0, a2, g2, k_k, k_a, r_k, vres):
    L = zr.shape[0]
    T = min(128, L)
    D = D_RWKV
    row = lambda i: (i, 0)
    const = lambda i: (0, 0)
    vec = lambda a: a.reshape(1, -1)
    has_vres = vres is not None
    ins = [zr, zr, vec(mu), vec(w0), w2, vec(a0), a2, g2, vec(k_k), vec(k_a), vec(r_k)]
    in_specs = [pl.BlockSpec((T, ZR_WIDTH), row),
                pl.BlockSpec((8, ZR_WIDTH), lambda i: (jnp.maximum(i * (T // 8) - 1, 0), 0)),
                pl.BlockSpec((1, ZR_WIDTH), const),
                pl.BlockSpec((1, D), const), pl.BlockSpec(w2.shape, const),
                pl.BlockSpec((1, D), const), pl.BlockSpec(a2.shape, const),
                pl.BlockSpec(g2.shape, const),
                pl.BlockSpec((1, D), const), pl.BlockSpec((1, D), const), pl.BlockSpec((1, D), const)]
    if has_vres:
        v_first, v_up, v_bias = vres
        ins += [v_first, v_up, vec(v_bias)]
        in_specs += [pl.BlockSpec((T, D), row), pl.BlockSpec((LANES, D), const),
                     pl.BlockSpec((1, D), const)]
    big = jax.ShapeDtypeStruct((L, D), F32)
    return pl.pallas_call(
        functools.partial(_rwkv_prep_kernel, T=T, has_vres=has_vres),
        grid=(L // T,),
        in_specs=in_specs,
        out_specs=[pl.BlockSpec((T, D), row)] * 7
                  + [pl.BlockSpec((T // CHUNK, 1, D), lambda i: (i, 0, 0))],
        out_shape=[big] * 7 + [jax.ShapeDtypeStruct((L // CHUNK, 1, D), F32)],
        compiler_params=_params(("parallel",)),
        name="rwkv_prep",
    )(*ins)


def _stack2(x, lo):
    zero = jnp.zeros_like(x)
    return jnp.concatenate([jnp.where(lo, x, zero), jnp.where(lo, zero, x)], axis=0)


def _scan_a_kernel(rt_ref, at_ref, bt_ref, kt_ref, v_ref, pc_ref, rh_ref, y0_ref, g_ref, h_ref,
                   *, CB, UNROLL):
    C = CHUNK
    lane = lax.broadcasted_iota(jnp.int32, (C, LANES), 1)
    lo = lane < RWKV_HEAD
    spos = lane % RWKV_HEAD
    tpos = lax.broadcasted_iota(jnp.int32, (C, LANES), 0)
    strict = spos < tpos
    incl = spos <= tpos
    eye_c = jnp.where(spos == tpos, 1.0, 0.0)
    ri = lax.broadcasted_iota(jnp.int32, (LANES, LANES), 0)
    ci = lax.broadcasted_iota(jnp.int32, (LANES, LANES), 1)
    same_head = (ri // RWKV_HEAD) == (ci // RWKV_HEAD)
    diag = ri == ci

    def stk(x):
        return _stack2(x, lo).astype(BF16)

    nt = lambda a, b: lax.dot_general(a, b, NT_DIMS, preferred_element_type=F32)
    tn = lambda a, b: lax.dot_general(a, b, TN_DIMS, preferred_element_type=F32)

    def each(fn, *cols):
        return [fn(*args) for args in zip(*cols)]

    def body(cg, carry):
        cs = [cg * UNROLL + u for u in range(UNROLL)]
        rows = [pl.ds(pl.multiple_of(c * C, C), C) for c in cs]
        rt = [rt_ref[r, :] for r in rows]
        at = [at_ref[r, :] for r in rows]
        bt = [bt_ref[r, :] for r in rows]
        kt = [kt_ref[r, :] for r in rows]
        v = [v_ref[r, :] for r in rows]
        pc = [pc_ref[c] for c in cs]
        at_b = each(lambda x: x.astype(BF16), at)
        rt_b = each(lambda x: x.astype(BF16), rt)
        bs = each(stk, bt)
        ks = each(stk, kt)
        vs = each(stk, v)
        m_ab = each(lambda a, b: jnp.where(strict, nt(a, b), 0.0), at_b, bs)
        m_ak = each(lambda a, b: jnp.where(strict, nt(a, b), 0.0), at_b, ks)
        m_rb = each(lambda a, b: jnp.where(incl, nt(a, b), 0.0), rt_b, bs)
        m_rk = each(lambda a, b: jnp.where(incl, nt(a, b), 0.0), rt_b, ks)
        mv = each(lambda m, x: _dot(m.astype(BF16), x), m_ak, vs)
        pw = m_ab
        inv = each(lambda m: eye_c + m, pw)
        for _ in range(5):
            pw = each(lambda x: _dot(x.astype(BF16), stk(x)), pw)
            inv = each(lambda t, x: t + _dot(t.astype(BF16), stk(x)), inv, pw)
        inv_b = each(lambda x: x.astype(BF16), inv)
        w = each(lambda t, x: _dot(t, stk(x)), inv_b, at)
        u0 = each(lambda t, x: _dot(t, stk(x)), inv_b, mv)
        m_rb_b = each(lambda x: x.astype(BF16), m_rb)
        rh = each(lambda r, m, x: r + _dot(m, stk(x)), rt, m_rb_b, w)
        y0 = each(lambda m, x, m2, x2: _dot(m, stk(x)) + _dot(m2.astype(BF16), x2),
                  m_rb_b, u0, m_rk, vs)
        btp = each(lambda x, p: (x * p).astype(BF16), bt, pc)
        ktp = each(lambda x, p: (x * p).astype(BF16), kt, pc)
        gm = each(lambda b, x, p: jnp.where(diag, jnp.broadcast_to(p, (LANES, LANES)), 0.0)
                  + jnp.where(same_head, tn(b, x.astype(BF16)), 0.0), btp, w, pc)
        hm = each(lambda b, x, k, y: jnp.where(same_head, tn(b, x.astype(BF16))
                                               + tn(k, y.astype(BF16)), 0.0), btp, u0, ktp, v)
        for u in range(UNROLL):
            rh_ref[rows[u], :] = rh[u]
            y0_ref[rows[u], :] = y0[u]
            g_ref[cs[u], 0] = gm[u]
            h_ref[cs[u], 0] = hm[u]
        return carry

    lax.fori_loop(0, CB // UNROLL, body, 0)


def _scan_a(rt, at, bt, kt, v, pc):
    L, D = rt.shape
    NP = D // LANES
    NC = L // CHUNK
    CB = min(16, NC)
    blk = pl.BlockSpec((CB * CHUNK, LANES), lambda ci, p: (ci, p))
    gh = pl.BlockSpec((CB, 1, LANES, LANES), lambda ci, p: (ci, p, 0, 0))
    return pl.pallas_call(
        functools.partial(_scan_a_kernel, CB=CB, UNROLL=min(16, CB)),
        grid=(NC // CB, NP),
        in_specs=[blk] * 5 + [pl.BlockSpec((CB, 1, LANES), lambda ci, p: (ci, 0, p))],
        out_specs=[blk, blk, gh, gh],
        out_shape=[jax.ShapeDtypeStruct((L, D), F32), jax.ShapeDtypeStruct((L, D), F32),
                   jax.ShapeDtypeStruct((NC, NP, LANES, LANES), F32),
                   jax.ShapeDtypeStruct((NC, NP, LANES, LANES), F32)],
        compiler_params=_params(("parallel", "parallel")),
        name="rwkv_chunk_local",
    )(rt, at, bt, kt, v, pc)


def _scan_b_kernel(rh_ref, y0_ref, g_ref, h_ref, bonus_ref, gate_ref, lg_ref, lb_ref, o_ref,
                   st_ref, *, CB, PP):
    ci = pl.program_id(1)

    @pl.when(ci == 0)
    def _():
        st_ref[...] = jnp.zeros(st_ref.shape, F32)

    li = lax.broadcasted_iota(jnp.int32, (LANES, LANES), 0) // RWKV_HEAD
    lj = lax.broadcasted_iota(jnp.int32, (LANES, LANES), 1) // RWKV_HEAD
    e_mean = jnp.where(li == lj, 1.0 / RWKV_HEAD, 0.0).astype(BF16)

    def body(c, carry):
        rows = pl.ds(pl.multiple_of(c * CHUNK, CHUNK), CHUNK)
        sls = [slice(p * LANES, (p + 1) * LANES) for p in range(PP)]

        def each(fn, *cols):
            return [fn(*args) for args in zip(*cols)]

        st = [st_ref[p] for p in range(PP)]
        y = each(lambda sl, s: _dot3(rh_ref[rows, sl], s) + y0_ref[rows, sl], sls, st)
        st_new = each(lambda p, s: _dot3(g_ref[c, p], s) + h_ref[c, p], list(range(PP)), st)
        for p in range(PP):
            st_ref[p] = st_new[p]
        mean = each(lambda x: _dot_exact_rhs(x, e_mean), y)
        d = each(lambda x, m: x - m, y, mean)
        var = each(lambda x: _dot_exact_rhs(x * x, e_mean), d)
        for p in range(PP):
            sl = sls[p]
            yn = d[p] * lax.rsqrt(var[p] + RWKV_GN_EPS) * lg_ref[:, sl] + lb_ref[:, sl]
            o_ref[rows, sl] = ((yn + bonus_ref[rows, sl]) * gate_ref[rows, sl]).astype(o_ref.dtype)
        return carry

    lax.fori_loop(0, CB, body, 0)


def _scan_b(rh, y0, g, h, bonus, gate, ln_g, ln_b):
    L, D = rh.shape
    NP = D // LANES
    NC = L // CHUNK
    CB = min(4, NC)
    PP = 16
    blk = pl.BlockSpec((CB * CHUNK, PP * LANES), lambda pg, ci: (ci, pg))
    gh = pl.BlockSpec((CB, PP, LANES, LANES), lambda pg, ci: (ci, pg, 0, 0))
    vec = pl.BlockSpec((1, PP * LANES), lambda pg, ci: (0, pg))
    return pl.pallas_call(
        functools.partial(_scan_b_kernel, CB=CB, PP=PP),
        grid=(NP // PP, NC // CB),
        in_specs=[blk, blk, gh, gh, blk, blk, vec, vec],
        out_specs=blk,
        out_shape=jax.ShapeDtypeStruct((L, D), BF16),
        scratch_shapes=[pltpu.VMEM((PP, LANES, LANES), F32)],
        compiler_params=_params(("parallel", "arbitrary")),
        name="rwkv_state_scan",
    )(rh, y0, g, h, bonus, gate, ln_g.reshape(1, -1), ln_b.reshape(1, -1))


def _ct_kernel(w_ref, o_ref):
    o_ref[...] = w_ref[0].T.astype(BF16)


def _ct_patch_kernel(w_ref, patch_ref, o_ref, *, first_patch_row):
    w = w_ref[0]
    rows = lax.broadcasted_iota(jnp.int32, w.shape, 0)
    last = pl.program_id(0) == pl.num_programs(0) - 1
    w = jnp.where(last & (rows >= first_patch_row), patch_ref[...], w)
    o_ref[...] = w.T.astype(BF16)


def _cast_transpose(w_t, l, n_cols, tn, row_of_tile, name, patch=None, first_patch_row=0):
    K = w_t.shape[2]
    in_specs = [pl.BlockSpec((pl.Element(1), pl.Element(tn), pl.Element(K)),
                             lambda j: (l, pl.multiple_of(row_of_tile(j), 16), 0))]
    args = [w_t]
    body = _ct_kernel
    if patch is not None:
        in_specs.append(pl.BlockSpec((tn, K), lambda j: (0, 0)))
        args.append(patch)
        body = functools.partial(_ct_patch_kernel, first_patch_row=first_patch_row)
    return pl.pallas_call(
        body,
        grid=(n_cols // tn,),
        in_specs=in_specs,
        out_specs=pl.BlockSpec((K, tn), lambda j: (0, j)),
        out_shape=jax.ShapeDtypeStruct((K, n_cols), BF16),
        compiler_params=_params(("parallel",)),
        name=name,
    )(*args)


def _pack_in_proj(w_t, l, vdown):
    tn = 512
    n_plain = IN_IDX // tn
    w_b = _cast_transpose(
        w_t, l, ZB_WIDTH, tn,
        lambda j: jnp.where(j < n_plain, j * tn, IN_GATE + (j - n_plain) * tn), "pack_w_b")
    n_own = ZR_XV - (ZR_WIDTH - tn)
    patch = jnp.pad(vdown.T, ((n_own, 0), (0, 0)))
    w_r = _cast_transpose(w_t, l, ZR_WIDTH, tn, lambda j: IN_RWKV + j * tn, "pack_w_r",
                          patch=patch, first_patch_row=n_own)
    w_i = _cast_transpose(w_t, l, ZI_WIDTH, 256, lambda j: IN_IDX + j * 256, "pack_w_i")
    return w_b, w_r, w_i


def _rows_at(w, start, height):
    return jnp.pad(w, ((start, height - start - w.shape[0]), (0, 0)))


def kernel(x, w_in, norm_mix, dw_weight, dw_bias, conv_ln_g, conv_ln_b, w_conv_out, w_att_out, rwkv_mu, rwkv_w0, rwkv_w2, rwkv_a0, rwkv_a2, rwkv_g2, rwkv_k_k, rwkv_k_a, rwkv_r_k, rwkv_ln_g, rwkv_ln_b, vres_down, vres_mu, vres_up, vres_bias, w_rwkv_out, w_out, norm_ffn, w_ffn_gate, w_ffn_up, w_ffn_down, norm_final):
    B, L, D = x.shape
    assert B == 1 and D == D_MODEL and L % 256 == 0
    depth = w_in.shape[0]
    xs = x.reshape(L, D)
    assert w_in.shape[1:] == (D_MODEL, D_IN)
    w_t = jnp.swapaxes(w_in, 1, 2)
    wc_b, wa_b, wr_b, wo_b = (w.astype(BF16) for w in (w_conv_out, w_att_out, w_rwkv_out, w_out))
    wg_b, wu_b, wd_b = (w.astype(BF16) for w in (w_ffn_gate, w_ffn_up, w_ffn_down))
    v_first = None
    for l in range(depth):
        vdown = vres_down[l - 1] if l > 0 else jnp.zeros((D, LORA_MV), F32)
        w_b, w_r, w_i = _pack_in_proj(w_t, l, vdown)
        mu_p = jnp.concatenate([rwkv_mu[l], vres_mu[l - 1] if l > 0 else jnp.zeros((LORA_MV,), F32)])
        w2_p = _rows_at(rwkv_w2[l], 0, LANES).astype(BF16)
        a2_p = _rows_at(rwkv_a2[l], ZR_XA - ZR_XA_WIN[0], ZR_XA_WIN[1] - ZR_XA_WIN[0]).astype(BF16)
        g2_p = _rows_at(rwkv_g2[l], ZR_XG - ZR_XG_WIN[0], ZR_XG_WIN[1] - ZR_XG_WIN[0]).astype(BF16)

        h = _rms_norm(xs, norm_mix[l], BF16)
        zb = _matmul(h, w_b, BF16, 1024, 512, "in_proj_bf16")
        zr = _matmul(h, w_r, F32, 1024, 512, "in_proj_rwkv")
        zi = _matmul(h, w_i, F32, 1024, 256, "in_proj_index")

        a_mix = _conformer_conv(zb, dw_weight[l], dw_bias[l], conv_ln_g[l], conv_ln_b[l])

        qr, kr, iqr, ik2, iws = _att_prep(zb, zi)
        b_mix = _dsa_attention(qr, iqr, iws, ik2, kr, zb)

        vres = None
        if l > 0:
            vup_p = _rows_at(vres_up[l - 1], ZR_XV - ZR_XV_WIN[0], LANES).astype(BF16)
            vres = (v_first, vup_p, vres_bias[l - 1])
        rt, at, bt, kt, v_rwkv, bonus, gate, pc = _rwkv_prep(
            zr, mu_p, rwkv_w0[l], w2_p, rwkv_a0[l], a2_p, g2_p,
            rwkv_k_k[l], rwkv_k_a[l], rwkv_r_k[l].reshape(-1), vres)
        if l == 0:
            v_first = v_rwkv
        rh, y0, g_mat, h_mat = _scan_a(rt, at, bt, kt, v_rwkv, pc)
        c_mix = _scan_b(rh, y0, g_mat, h_mat, bonus, gate, rwkv_ln_g[l], rwkv_ln_b[l])

        merged = _merge(a_mix, b_mix, c_mix, wc_b, wa_b, wr_b, l, zb, 1024, 512)
        xs = _matmul_residual(merged, wo_b, l, xs, 1024, 512, "out_proj")

        h2 = _rms_norm(xs, norm_ffn[l], BF16)
        act = _ffn1(h2, wg_b, wu_b, l, 1024, 256)
        xs = _matmul_residual(act, wd_b, l, xs, 512, 256, "ffn_down")
    return _rms_norm(xs, norm_final, F32).reshape(B, L, D)
```

```python
import functools

import jax
import jax.numpy as jnp
import numpy as np
from jax import lax
from jax.experimental import pallas as pl
from jax.experimental.pallas import tpu as pltpu

F32 = jnp.float32
BF16 = jnp.bfloat16

D_MODEL = 4096
RMS_EPS = 1e-6
LN_EPS = 1e-5
D_CONV = D_MODEL // 4
CONV_WIDTH = 31
ATT_HEADS = 8
ATT_HEAD_DIM = 128
D_ATT = ATT_HEADS * ATT_HEAD_DIM
ATT_SCALE = ATT_HEAD_DIM ** -0.5
LOG2_E = 1.4426950408889634
IDX_HEADS = 16
IDX_HEAD_DIM = 64
IDX_SCALE = IDX_HEAD_DIM ** -0.5
IDX_W_SCALE = IDX_HEADS ** -0.5
TOPK_MAX = 256
ROPE_THETA = 500000.0
ROPE_FRACTION = 4
D_RWKV = D_MODEL // 2
RWKV_HEAD = 64
LORA_DECAY = 96
LORA_AAA = 96
LORA_MV = 64
LORA_GATE = 256
RWKV_GN_EPS = 64e-5
D_FF = ((8 * D_MODEL + 3 * 256 - 1) // (3 * 256)) * 256

LANES = 128
CHUNK = 64
IN_ATT = 2 * D_CONV
IN_IDX = IN_ATT + 3 * D_ATT
IN_RWKV = IN_IDX + IDX_HEADS * IDX_HEAD_DIM + IDX_HEAD_DIM + IDX_HEADS
IN_GATE = IN_RWKV + 3 * D_RWKV + LORA_DECAY + LORA_AAA + LORA_GATE
D_IN = IN_GATE + 3 * D_MODEL
ZR_XW = 3 * D_RWKV
ZR_XA = ZR_XW + LORA_DECAY
ZR_XG = ZR_XA + LORA_AAA
ZR_XV = ZR_XG + LORA_GATE
ZR_WIDTH = ZR_XV + LORA_MV
ZR_XA_WIN = (ZR_XW, ZR_XW + 2 * LANES)
ZR_XG_WIN = (ZR_XW + LANES, ZR_WIDTH)
ZR_XV_WIN = (ZR_WIDTH - LANES, ZR_WIDTH)
ZI_IK = IDX_HEADS * IDX_HEAD_DIM
ZI_IW_LANE = IDX_HEAD_DIM
ZI_WIDTH = ZI_IK + 2 * LANES
ZB_GATE = 2 * D_CONV + 3 * D_ATT
ZB_WIDTH = ZB_GATE + 3 * D_MODEL
VMEM_LIMIT = 56 * 1024 * 1024
INT_MIN = -2147483648
NEG_BIG = -1e30

NT_DIMS = (((1,), (1,)), ((), ()))
TN_DIMS = (((0,), (0,)), ((), ()))


def _params(sem, vmem=VMEM_LIMIT):
    return pltpu.CompilerParams(dimension_semantics=sem, vmem_limit_bytes=vmem)


def _dot(a, b):
    return jnp.dot(a, b, preferred_element_type=F32)


def _split2(x):
    hi = x.astype(BF16)
    lo = (x - hi.astype(F32)).astype(BF16)
    return hi, lo


def _dot_exact_rhs(x, e):
    h1 = x.astype(BF16)
    r1 = x - h1.astype(F32)
    h2 = r1.astype(BF16)
    h3 = (r1 - h2.astype(F32)).astype(BF16)
    return _dot(h1, e) + _dot(h2, e) + _dot(h3, e)


def _dot_exact_lhs(e, x):
    h1 = x.astype(BF16)
    r1 = x - h1.astype(F32)
    h2 = r1.astype(BF16)
    h3 = (r1 - h2.astype(F32)).astype(BF16)
    return _dot(e, h1) + _dot(e, h2) + _dot(e, h3)


def _dot3(a, b):
    ah, al = _split2(a)
    bh, bl = _split2(b)
    return _dot(ah, bh) + _dot(ah, bl) + _dot(al, bh)


def _sigmoid(x):
    return 1.0 / (1.0 + jnp.exp(-x))


def _rms_kernel(x_ref, g_ref, o_ref):
    x = x_ref[...]
    y = x * lax.rsqrt(jnp.mean(x * x, axis=-1, keepdims=True) + RMS_EPS)
    o_ref[...] = (y * g_ref[...]).astype(o_ref.dtype)


def _rms_norm(x, g, out_dtype):
    L, D = x.shape
    tr = min(256, L)
    return pl.pallas_call(
        _rms_kernel,
        grid=(L // tr,),
        in_specs=[pl.BlockSpec((tr, D), lambda i: (i, 0)),
                  pl.BlockSpec((1, D), lambda i: (0, 0))],
        out_specs=pl.BlockSpec((tr, D), lambda i: (i, 0)),
        out_shape=jax.ShapeDtypeStruct((L, D), out_dtype),
        compiler_params=_params(("parallel",)),
        name="rms_norm",
    )(x, g.reshape(1, D))


def _mm_kernel(a_ref, b_ref, o_ref):
    o_ref[...] = _dot(a_ref[...], b_ref[...]).astype(o_ref.dtype)


def _matmul(a, b, out_dtype, tm, tn, name):
    M, K = a.shape
    N = b.shape[1]
    tm = min(tm, M)
    return pl.pallas_call(
        _mm_kernel,
        grid=(M // tm, N // tn),
        in_specs=[pl.BlockSpec((tm, K), lambda i, j: (i, 0)),
                  pl.BlockSpec((K, tn), lambda i, j: (0, j))],
        out_specs=pl.BlockSpec((tm, tn), lambda i, j: (i, j)),
        out_shape=jax.ShapeDtypeStruct((M, N), out_dtype),
        compiler_params=_params(("parallel", "parallel")),
        name=name,
    )(a, b)


def _mm_res_kernel(a_ref, b_ref, x_ref, o_ref):
    o_ref[...] = x_ref[...] + _dot(a_ref[...], b_ref[...])


def _layer_weight_spec(w, l, tn):
    return pl.BlockSpec((None, w.shape[1], tn), lambda i, j: (l, 0, j))


def _matmul_residual(a, b, l, x, tm, tn, name):
    M, K = a.shape
    N = b.shape[2]
    tm = min(tm, M)
    return pl.pallas_call(
        _mm_res_kernel,
        grid=(M // tm, N // tn),
        in_specs=[pl.BlockSpec((tm, K), lambda i, j: (i, 0)),
                  _layer_weight_spec(b, l, tn),
                  pl.BlockSpec((tm, tn), lambda i, j: (i, j))],
        out_specs=pl.BlockSpec((tm, tn), lambda i, j: (i, j)),
        out_shape=jax.ShapeDtypeStruct((M, N), F32),
        compiler_params=_params(("parallel", "parallel")),
        name=name,
    )(a, b, x)


def _ffn1_kernel(h_ref, wg_ref, wu_ref, o_ref):
    h = h_ref[...]
    g = _dot(h, wg_ref[...])
    u = _dot(h, wu_ref[...])
    o_ref[...] = (g * _sigmoid(g) * u).astype(o_ref.dtype)


def _ffn1(h, wg, wu, l, tm, tn):
    M, K = h.shape
    N = wg.shape[2]
    tm = min(tm, M)
    return pl.pallas_call(
        _ffn1_kernel,
        grid=(M // tm, N // tn),
        in_specs=[pl.BlockSpec((tm, K), lambda i, j: (i, 0)),
                  _layer_weight_spec(wg, l, tn),
                  _layer_weight_spec(wu, l, tn)],
        out_specs=pl.BlockSpec((tm, tn), lambda i, j: (i, j)),
        out_shape=jax.ShapeDtypeStruct((M, N), BF16),
        compiler_params=_params(("parallel", "parallel")),
        name="ffn_gate_up",
    )(h, wg, wu)


def _merge_kernel(a_ref, b_ref, c_ref, wa_ref, wb_ref, wc_ref, g0_ref, g1_ref, g2_ref, o_ref):
    ya = _dot(a_ref[...], wa_ref[...])
    yb = _dot(b_ref[...], wb_ref[...])
    yc = _dot(c_ref[...], wc_ref[...])
    m = (_sigmoid(g0_ref[...].astype(F32)) * ya + _sigmoid(g1_ref[...].astype(F32)) * yb
         + _sigmoid(g2_ref[...].astype(F32)) * yc)
    o_ref[...] = m.astype(o_ref.dtype)


def _merge(a, b, c, wa, wb, wc, l, zb, tm, tn):
    M = a.shape[0]
    tm = min(tm, M)
    g_off = ZB_GATE // tn
    g_stride = D_MODEL // tn
    return pl.pallas_call(
        _merge_kernel,
        grid=(M // tm, D_MODEL // tn),
        in_specs=[pl.BlockSpec((tm, D_CONV), lambda i, j: (i, 0)),
                  pl.BlockSpec((tm, D_ATT), lambda i, j: (i, 0)),
                  pl.BlockSpec((tm, D_RWKV), lambda i, j: (i, 0)),
                  _layer_weight_spec(wa, l, tn),
                  _layer_weight_spec(wb, l, tn),
                  _layer_weight_spec(wc, l, tn),
                  pl.BlockSpec((tm, tn), lambda i, j: (i, g_off + j)),
                  pl.BlockSpec((tm, tn), lambda i, j: (i, g_off + g_stride + j)),
                  pl.BlockSpec((tm, tn), lambda i, j: (i, g_off + 2 * g_stride + j))],
        out_specs=pl.BlockSpec((tm, tn), lambda i, j: (i, j)),
        out_shape=jax.ShapeDtypeStruct((M, D_MODEL), BF16),
        compiler_params=_params(("parallel", "parallel")),
        name="gated_merge",
    )(a, b, c, wa, wb, wc, zb, zb, zb)


def _conv_kernel(val_ref, gate_ref, hval_ref, hgate_ref, w_ref, b_ref, lg_ref, lb_ref, o_ref,
                 c_ref, *, T):
    i = pl.program_id(0)
    halo = 32
    hv = hval_ref[...].astype(F32)
    hg = hgate_ref[...].astype(F32)
    hc = hv * _sigmoid(hg)
    c_ref[0:halo, :] = jnp.where(i > 0, hc, 0.0)
    v = val_ref[...].astype(F32)
    g = gate_ref[...].astype(F32)
    c_ref[halo:halo + T, :] = v * _sigmoid(g)
    acc = jnp.zeros((T, D_CONV), F32) + b_ref[...]
    for j in range(CONV_WIDTH):
        off = halo - (CONV_WIDTH - 1) + j
        acc = acc + c_ref[off:off + T, :] * w_ref[j:j + 1, :]
    m = jnp.mean(acc, axis=-1, keepdims=True)
    d = acc - m
    var = jnp.mean(d * d, axis=-1, keepdims=True)
    y = d * lax.rsqrt(var + LN_EPS) * lg_ref[...] + lb_ref[...]
    o_ref[...] = (y * _sigmoid(y)).astype(o_ref.dtype)


def _conformer_conv(zb, dw_w, dw_b, ln_g, ln_b):
    L = zb.shape[0]
    T = min(256, L)
    hb = T // 32
    w_pad = jnp.zeros((32, D_CONV), F32).at[:CONV_WIDTH].set(dw_w)
    return pl.pallas_call(
        functools.partial(_conv_kernel, T=T),
        grid=(L // T,),
        in_specs=[pl.BlockSpec((T, D_CONV), lambda i: (i, 0)),
                  pl.BlockSpec((T, D_CONV), lambda i: (i, 1)),
                  pl.BlockSpec((32, D_CONV), lambda i: (jnp.maximum(i * hb - 1, 0), 0)),
                  pl.BlockSpec((32, D_CONV), lambda i: (jnp.maximum(i * hb - 1, 0), 1)),
                  pl.BlockSpec((32, D_CONV), lambda i: (0, 0)),
                  pl.BlockSpec((1, D_CONV), lambda i: (0, 0)),
                  pl.BlockSpec((1, D_CONV), lambda i: (0, 0)),
                  pl.BlockSpec((1, D_CONV), lambda i: (0, 0))],
        out_specs=pl.BlockSpec((T, D_CONV), lambda i: (i, 0)),
        out_shape=jax.ShapeDtypeStruct((L, D_CONV), BF16),
        scratch_shapes=[pltpu.VMEM((T + 32, D_CONV), F32)],
        compiler_params=_params(("parallel",)),
        name="conformer_conv",
    )(zb, zb, zb, zb, w_pad, dw_b.reshape(1, -1), ln_g.reshape(1, -1), ln_b.reshape(1, -1))


def _rope_tables(pos, inv_freq, head_dim):
    rd = head_dim // ROPE_FRACTION
    half = rd // 2
    ang = pos * inv_freq
    cos = jnp.cos(ang)
    sin = jnp.sin(ang)
    lane = lax.broadcasted_iota(jnp.int32, ang.shape, 1) % head_dim
    c = jnp.where(lane < rd, cos, 1.0)
    s_up = jnp.where(lane < half, -sin, 0.0)
    s_dn = jnp.where((lane >= half) & (lane < rd), sin, 0.0)
    return c, s_up, s_dn


def _rope_apply(x, tabs, half):
    c, s_up, s_dn = tabs
    return (x * c + pltpu.roll(x, LANES - half, 1) * s_up + pltpu.roll(x, half, 1) * s_dn)


def _attprep_kernel(q_ref, k_ref, zi_ref, fq_ref, fi_ref, qo_ref, ko_ref, iqo_ref, iko_ref,
                    iwo_ref, *, T):
    i = pl.program_id(0)
    pos = (i * T + lax.broadcasted_iota(jnp.int32, (T, LANES), 0)).astype(F32)
    tq = _rope_tables(pos, fq_ref[...], ATT_HEAD_DIM)
    ti = _rope_tables(pos, fi_ref[...], IDX_HEAD_DIM)
    hq = ATT_HEAD_DIM // ROPE_FRACTION // 2
    hi = IDX_HEAD_DIM // ROPE_FRACTION // 2
    for h in range(ATT_HEADS):
        sl = slice(h * LANES, (h + 1) * LANES)
        qo_ref[:, sl] = (_rope_apply(q_ref[:, sl].astype(F32), tq, hq)
                         * (ATT_SCALE * LOG2_E)).astype(BF16)
        ko_ref[:, sl] = _rope_apply(k_ref[:, sl].astype(F32), tq, hq).astype(BF16)
    for p in range(ZI_IK // LANES):
        sl = slice(p * LANES, (p + 1) * LANES)
        iqo_ref[:, sl] = _rope_apply(zi_ref[:, sl], ti, hi).astype(BF16)
    grp = zi_ref[:, ZI_IK:ZI_IK + LANES]
    ik = _rope_apply(grp, ti, hi)
    lane = lax.broadcasted_iota(jnp.int32, (T, LANES), 1)
    iko_ref[...] = jnp.where(lane < IDX_HEAD_DIM, ik, pltpu.roll(ik, IDX_HEAD_DIM, 1)).astype(BF16)
    iwo_ref[...] = grp * (IDX_W_SCALE * IDX_SCALE)


def _lane_inv_freq(head_dim):
    rd = head_dim // ROPE_FRACTION
    half = rd // 2
    inv_freq = jnp.power(ROPE_THETA, -jnp.arange(half, dtype=F32) * (2.0 / rd))
    lane = np.arange(LANES) % head_dim
    idx = np.where(lane < rd, lane % half, 0)
    return jnp.where(jnp.asarray(lane < rd), inv_freq[idx], 0.0).reshape(1, LANES)


def _att_prep(zb, zi):
    L = zb.shape[0]
    T = min(256, L)
    row = lambda i: (i, 0)
    return pl.pallas_call(
        functools.partial(_attprep_kernel, T=T),
        grid=(L // T,),
        in_specs=[pl.BlockSpec((T, D_ATT), lambda i: (i, 2)),
                  pl.BlockSpec((T, D_ATT), lambda i: (i, 3)),
                  pl.BlockSpec((T, ZI_WIDTH), row),
                  pl.BlockSpec((1, LANES), lambda i: (0, 0)),
                  pl.BlockSpec((1, LANES), lambda i: (0, 0))],
        out_specs=[pl.BlockSpec((T, D_ATT), row), pl.BlockSpec((T, D_ATT), row),
                   pl.BlockSpec((T, ZI_IK), row), pl.BlockSpec((T, LANES), row),
                   pl.BlockSpec((T, LANES), row)],
        out_shape=[jax.ShapeDtypeStruct((L, D_ATT), BF16), jax.ShapeDtypeStruct((L, D_ATT), BF16),
                   jax.ShapeDtypeStruct((L, ZI_IK), BF16), jax.ShapeDtypeStruct((L, LANES), BF16),
                   jax.ShapeDtypeStruct((L, LANES), F32)],
        compiler_params=_params(("parallel",)),
        name="att_prep",
    )(zb, zb, zi, _lane_inv_freq(ATT_HEAD_DIM), _lane_inv_freq(IDX_HEAD_DIM))


def _dsa_kernel(q_ref, iq_ref, iw_ref, ik_ref, k_ref, v_ref, o_ref,
                keyt_ref, wt_ref, iqm_ref, thr_ref, cst_ref, m_ref, l_ref, acc_ref,
                *, Q, S, SLAB, topk):
    i = pl.program_id(0)
    j = pl.program_id(1)
    q_end = (i + 1) * Q

    @pl.when(j == 0)
    def _select():
        lane = lax.broadcasted_iota(jnp.int32, (Q, LANES), 1)
        lo = lane < IDX_HEAD_DIM
        zero = jnp.zeros((Q, LANES), BF16)
        for p in range(IDX_HEADS // 2):
            x = iq_ref[:, p * LANES:(p + 1) * LANES]
            iqm_ref[2 * p] = jnp.where(lo, x, zero)
            iqm_ref[2 * p + 1] = jnp.where(lo, zero, x)
        wt_ref[...] = iw_ref[...].T
        qpos = i * Q + lax.broadcasted_iota(jnp.int32, (S, Q), 1)

        def score_chunk(c, carry):
            off = pl.multiple_of(c * S, S)
            ikc = ik_ref[pl.ds(off, S), :]
            sc = jnp.zeros((S, Q), F32)
            for h in range(IDX_HEADS):
                lg = lax.dot_general(ikc, iqm_ref[h], NT_DIMS, preferred_element_type=F32)
                sc = sc + jnp.maximum(lg, 0.0) * wt_ref[ZI_IW_LANE + h:ZI_IW_LANE + h + 1, :]
            sc = jnp.where(sc == 0.0, 0.0, sc)
            bits = pltpu.bitcast(sc, jnp.int32)
            key = bits ^ ((bits >> 31) & 0x7FFFFFFF)
            kpos = off + lax.broadcasted_iota(jnp.int32, (S, Q), 0)
            keyt_ref[pl.ds(off, S), :] = jnp.where(kpos <= qpos, key, INT_MIN)
            return carry

        nchunk = (q_end + S - 1) // S
        lax.fori_loop(0, nchunk, score_chunk, 0)

        def count_keys(one_if_hit):
            def body(c, cnt):
                off = pl.multiple_of(c * S, S)
                for u in range(S // SLAB):
                    kc = keyt_ref[pl.ds(off + u * SLAB, SLAB), :]
                    cnt = cnt + one_if_hit(kc, off + u * SLAB)
                return cnt
            cnt = lax.fori_loop(0, nchunk, body, jnp.zeros((SLAB, Q), F32))
            return jnp.sum(cnt, axis=0, keepdims=True)

        def bit_step(b, carry):
            prefix, n_ge = carry
            cand_u = prefix | jnp.left_shift(jnp.int32(1), 31 - b)
            cand = jnp.broadcast_to(cand_u ^ INT_MIN, (SLAB, Q))
            tot = count_keys(lambda kc, _: jnp.where(kc >= cand, 1.0, 0.0))
            ok = tot >= float(topk)
            return jnp.where(ok, cand_u, prefix), jnp.where(ok, tot, n_ge)

        prefix, n_ge = lax.fori_loop(
            0, 32, bit_step, (jnp.zeros((1, Q), jnp.int32), jnp.zeros((1, Q), F32)))
        thr = jnp.maximum(prefix ^ INT_MIN, INT_MIN + 1)
        thr_ref[...] = jnp.broadcast_to(thr, thr_ref.shape)
        cst_ref[...] = jnp.full(cst_ref.shape, 2147483647, jnp.int32)

        @pl.when(jnp.max(n_ge) > float(topk))
        def _break_ties():
            thr_b = jnp.broadcast_to(thr, (SLAB, Q))
            need = float(topk) - count_keys(lambda kc, _: jnp.where(kc > thr_b, 1.0, 0.0))
            slab_pos = lax.broadcasted_iota(jnp.int32, (SLAB, Q), 0)
            nbits = max(1, (keyt_ref.shape[0] - 1).bit_length())

            def pos_step(b, ans):
                t = ans | jnp.left_shift(jnp.int32(1), nbits - 1 - b)
                lim = jnp.broadcast_to(t - 1, (SLAB, Q))
                below = count_keys(
                    lambda kc, p0: jnp.where(kc == thr_b,
                                             jnp.where(slab_pos + p0 <= lim, 1.0, 0.0), 0.0))
                return jnp.where(below < need, t, ans)

            cst = lax.fori_loop(0, nbits, pos_step, jnp.zeros((1, Q), jnp.int32))
            cst_ref[...] = jnp.broadcast_to(cst, cst_ref.shape)

        m_ref[...] = jnp.full(m_ref.shape, NEG_BIG, F32)
        l_ref[...] = jnp.zeros(l_ref.shape, F32)
        acc_ref[...] = jnp.zeros(acc_ref.shape, F32)

    @pl.when(j * S < q_end)
    def _attend():
        keys = keyt_ref[pl.ds(pl.multiple_of(j * S, S), S), :]
        kpos = j * S + lax.broadcasted_iota(jnp.int32, (S, Q), 0)
        thr = thr_ref[0:1, :]
        at_thr = jnp.where(kpos <= cst_ref[0:1, :], 0.0, NEG_BIG)
        bias = jnp.where(keys > thr, 0.0, jnp.where(keys == thr, at_thr, NEG_BIG)).T
        for h in range(ATT_HEADS):
            sl = slice(h * LANES, (h + 1) * LANES)
            s = lax.dot_general(q_ref[:, sl], k_ref[:, sl], NT_DIMS, preferred_element_type=F32)
            s = s + bias
            m_prev = m_ref[h]
            m_new = jnp.maximum(m_prev, jnp.max(s, axis=1, keepdims=True))
            alpha = jnp.exp2(m_prev - m_new)
            p = jnp.exp2(s - jnp.tile(m_new, (1, S // LANES)))
            l_ref[h] = alpha * l_ref[h] + jnp.sum(p, axis=1, keepdims=True)
            acc_ref[:, sl] = alpha * acc_ref[:, sl] + _dot(p.astype(BF16), v_ref[:, sl])
            m_ref[h] = m_new

    @pl.when(j == pl.num_programs(1) - 1)
    def _finish():
        for h in range(ATT_HEADS):
            sl = slice(h * LANES, (h + 1) * LANES)
            o_ref[:, sl] = (acc_ref[:, sl] / l_ref[h]).astype(o_ref.dtype)


def _dsa_attention(qr, iqr, iws, ik2, kr, zb):
    L = qr.shape[0]
    Q = min(512, L)
    S = min(512, L)
    SLAB = 32
    topk = min(TOPK_MAX, L // 4)
    nk = L // S

    def kv_block(i, j):
        return jnp.minimum(j, ((i + 1) * Q - 1) // S)

    return pl.pallas_call(
        functools.partial(_dsa_kernel, Q=Q, S=S, SLAB=SLAB, topk=topk),
        grid=(L // Q, nk),
        in_specs=[pl.BlockSpec((Q, D_ATT), lambda i, j: (i, 0)),
                  pl.BlockSpec((Q, ZI_IK), lambda i, j: (i, 0)),
                  pl.BlockSpec((Q, LANES), lambda i, j: (i, 0)),
                  pl.BlockSpec((L, LANES), lambda i, j: (0, 0)),
                  pl.BlockSpec((S, D_ATT), lambda i, j: (kv_block(i, j), 0)),
                  pl.BlockSpec((S, D_ATT), lambda i, j: (kv_block(i, j), 4))],
        out_specs=pl.BlockSpec((Q, D_ATT), lambda i, j: (i, 0)),
        out_shape=jax.ShapeDtypeStruct((L, D_ATT), BF16),
        scratch_shapes=[pltpu.VMEM((L, Q), jnp.int32),
                        pltpu.VMEM((LANES, Q), F32),
                        pltpu.VMEM((IDX_HEADS, Q, LANES), BF16),
                        pltpu.VMEM((8, Q), jnp.int32),
                        pltpu.VMEM((8, Q), jnp.int32),
                        pltpu.VMEM((ATT_HEADS, Q, LANES), F32),
                        pltpu.VMEM((ATT_HEADS, Q, LANES), F32),
                        pltpu.VMEM((Q, D_ATT), F32)],
        compiler_params=_params(("parallel", "arbitrary")),
        name="dsa_attention",
    )(qr, iqr, iws, ik2, kr, zb)


def _head_sum(x, e):
    cols = []
    for p in range(x.shape[1] // LANES):
        cols.append(_dot_exact_rhs(x[:, p * LANES:(p + 1) * LANES], e))
    return jnp.concatenate(cols, axis=1)


def _rwkv_prep_kernel(*refs, T, has_vres):
    if has_vres:
        (z_ref, halo_ref, mu_ref, w0_ref, w2_ref, a0_ref, a2_ref, g2_ref, kk_ref, ka_ref, rk_ref,
         vf_ref, vup_ref, vb_ref,
         rt_ref, at_ref, bt_ref, kt_ref, v_ref, bonus_ref, g_ref, pc_ref) = refs
    else:
        (z_ref, halo_ref, mu_ref, w0_ref, w2_ref, a0_ref, a2_ref, g2_ref, kk_ref, ka_ref, rk_ref,
         rt_ref, at_ref, bt_ref, kt_ref, v_ref, bonus_ref, g_ref, pc_ref) = refs
    i = pl.program_id(0)
    D = D_RWKV

    def shifted(lo, hi):
        z = z_ref[:, lo:hi]
        first = jnp.where(i > 0, halo_ref[7:8, lo:hi], 0.0)
        rows = lax.broadcasted_iota(jnp.int32, z.shape, 0)
        prev = jnp.where(rows == 0, first, pltpu.roll(z, 1, 0))
        return z + (prev - z) * mu_ref[:, lo:hi]

    r = shifted(0, D)
    kraw = shifted(D, 2 * D)
    v = shifted(2 * D, 3 * D)
    xw = shifted(ZR_XW, ZR_XW + LANES)
    xa = shifted(*ZR_XA_WIN)
    xg = shifted(*ZR_XG_WIN)
    if has_vres:
        xv = shifted(*ZR_XV_WIN)
        mix = _sigmoid(vb_ref[...] + _dot(xv.astype(BF16), vup_ref[...]))
        v = v + (vf_ref[...] - v) * mix
    v_ref[...] = v

    w_in = w0_ref[...] + _dot(jnp.tanh(xw).astype(BF16), w2_ref[...])
    w_log = -(jnp.maximum(-w_in, 0.0) + jnp.log(1.0 + jnp.exp(-jnp.abs(w_in)))) - 0.5
    logw = -jnp.exp(w_log)
    a_lr = _sigmoid(a0_ref[...] + _dot(xa.astype(BF16), a2_ref[...]))
    g_ref[...] = _dot(_sigmoid(xg).astype(BF16), g2_ref[...])

    li = lax.broadcasted_iota(jnp.int32, (LANES, LANES), 0) // RWKV_HEAD
    lj = lax.broadcasted_iota(jnp.int32, (LANES, LANES), 1) // RWKV_HEAD
    e_head = jnp.where(li == lj, 1.0, 0.0).astype(BF16)
    kk = kraw * kk_ref[...]
    norm = jnp.sqrt(_head_sum(kk * kk, e_head))
    kk = kk / jnp.maximum(norm, 1e-12)
    k = kraw * (1.0 + (a_lr - 1.0) * ka_ref[...])
    bonus_ref[...] = _head_sum(r * k * rk_ref[...], e_head) * v

    ti = lax.broadcasted_iota(jnp.int32, (T, T), 0)
    tj = lax.broadcasted_iota(jnp.int32, (T, T), 1)
    tri = jnp.where((ti // CHUNK == tj // CHUNK) & (tj <= ti), 1.0, 0.0).astype(BF16)
    cum = _dot_exact_lhs(tri, logw)
    p_in = jnp.exp(cum)
    p_out = jnp.exp(-cum)
    rt_ref[...] = r * p_in
    at_ref[...] = -kk * jnp.exp(cum - logw)
    bt_ref[...] = kk * a_lr * p_out
    kt_ref[...] = k * p_out
    for c in range(T // CHUNK):
        pc_ref[c] = p_in[(c + 1) * CHUNK - 1:(c + 1) * CHUNK, :]


def _rwkv_prep(zr, mu, w0, w2, a0, a2, g2, k_k, k_a, r_k, vres):
    L = zr.shape[0]
    T = min(128, L)
    D = D_RWKV
    row = lambda i: (i, 0)
    const = lambda i: (0, 0)
    vec = lambda a: a.reshape(1, -1)
    has_vres = vres is not None
    ins = [zr, zr, vec(mu), vec(w0), w2, vec(a0), a2, g2, vec(k_k), vec(k_a), vec(r_k)]
    in_specs = [pl.BlockSpec((T, ZR_WIDTH), row),
                pl.BlockSpec((8, ZR_WIDTH), lambda i: (jnp.maximum(i * (T // 8) - 1, 0), 0)),
                pl.BlockSpec((1, ZR_WIDTH), const),
                pl.BlockSpec((1, D), const), pl.BlockSpec(w2.shape, const),
                pl.BlockSpec((1, D), const), pl.BlockSpec(a2.shape, const),
                pl.BlockSpec(g2.shape, const),
                pl.BlockSpec((1, D), const), pl.BlockSpec((1, D), const), pl.BlockSpec((1, D), const)]
    if has_vres:
        v_first, v_up, v_bias = vres
        ins += [v_first, v_up, vec(v_bias)]
        in_specs += [pl.BlockSpec((T, D), row), pl.BlockSpec((LANES, D), const),
                     pl.BlockSpec((1, D), const)]
    big = jax.ShapeDtypeStruct((L, D), F32)
    return pl.pallas_call(
        functools.partial(_rwkv_prep_kernel, T=T, has_vres=has_vres),
        grid=(L // T,),
        in_specs=in_specs,
        out_specs=[pl.BlockSpec((T, D), row)] * 7
                  + [pl.BlockSpec((T // CHUNK, 1, D), lambda i: (i, 0, 0))],
        out_shape=[big] * 7 + [jax.ShapeDtypeStruct((L // CHUNK, 1, D), F32)],
        compiler_params=_params(("parallel",)),
        name="rwkv_prep",
    )(*ins)


def _stack2(x, lo):
    zero = jnp.zeros_like(x)
    return jnp.concatenate([jnp.where(lo, x, zero), jnp.where(lo, zero, x)], axis=0)


def _scan_a_kernel(rt_ref, at_ref, bt_ref, kt_ref, v_ref, pc_ref, rh_ref, y0_ref, g_ref, h_ref,
                   *, CB, UNROLL):
    C = CHUNK
    lane = lax.broadcasted_iota(jnp.int32, (C, LANES), 1)
    lo = lane < RWKV_HEAD
    spos = lane % RWKV_HEAD
    tpos = lax.broadcasted_iota(jnp.int32, (C, LANES), 0)
    strict = spos < tpos
    incl = spos <= tpos
    eye_c = jnp.where(spos == tpos, 1.0, 0.0)
    ri = lax.broadcasted_iota(jnp.int32, (LANES, LANES), 0)
    ci = lax.broadcasted_iota(jnp.int32, (LANES, LANES), 1)
    same_head = (ri // RWKV_HEAD) == (ci // RWKV_HEAD)
    diag = ri == ci

    def stk(x):
        return _stack2(x, lo).astype(BF16)

    nt = lambda a, b: lax.dot_general(a, b, NT_DIMS, preferred_element_type=F32)
    tn = lambda a, b: lax.dot_general(a, b, TN_DIMS, preferred_element_type=F32)

    def each(fn, *cols):
        return [fn(*args) for args in zip(*cols)]

    def body(cg, carry):
        cs = [cg * UNROLL + u for u in range(UNROLL)]
        rows = [pl.ds(pl.multiple_of(c * C, C), C) for c in cs]
        rt = [rt_ref[r, :] for r in rows]
        at = [at_ref[r, :] for r in rows]
        bt = [bt_ref[r, :] for r in rows]
        kt = [kt_ref[r, :] for r in rows]
        v = [v_ref[r, :] for r in rows]
        pc = [pc_ref[c] for c in cs]
        at_b = each(lambda x: x.astype(BF16), at)
        rt_b = each(lambda x: x.astype(BF16), rt)
        bs = each(stk, bt)
        ks = each(stk, kt)
        vs = each(stk, v)
        m_ab = each(lambda a, b: jnp.where(strict, nt(a, b), 0.0), at_b, bs)
        m_ak = each(lambda a, b: jnp.where(strict, nt(a, b), 0.0), at_b, ks)
        m_rb = each(lambda a, b: jnp.where(incl, nt(a, b), 0.0), rt_b, bs)
        m_rk = each(lambda a, b: jnp.where(incl, nt(a, b), 0.0), rt_b, ks)
        mv = each(lambda m, x: _dot(m.astype(BF16), x), m_ak, vs)
        pw = m_ab
        inv = each(lambda m: eye_c + m, pw)
        for _ in range(5):
            pw = each(lambda x: _dot(x.astype(BF16), stk(x)), pw)
            inv = each(lambda t, x: t + _dot(t.astype(BF16), stk(x)), inv, pw)
        inv_b = each(lambda x: x.astype(BF16), inv)
        w = each(lambda t, x: _dot(t, stk(x)), inv_b, at)
        u0 = each(lambda t, x: _dot(t, stk(x)), inv_b, mv)
        m_rb_b = each(lambda x: x.astype(BF16), m_rb)
        rh = each(lambda r, m, x: r + _dot(m, stk(x)), rt, m_rb_b, w)
        y0 = each(lambda m, x, m2, x2: _dot(m, stk(x)) + _dot(m2.astype(BF16), x2),
                  m_rb_b, u0, m_rk, vs)
        btp = each(lambda x, p: (x * p).astype(BF16), bt, pc)
        ktp = each(lambda x, p: (x * p).astype(BF16), kt, pc)
        gm = each(lambda b, x, p: jnp.where(diag, jnp.broadcast_to(p, (LANES, LANES)), 0.0)
                  + jnp.where(same_head, tn(b, x.astype(BF16)), 0.0), btp, w, pc)
        hm = each(lambda b, x, k, y: jnp.where(same_head, tn(b, x.astype(BF16))
                                               + tn(k, y.astype(BF16)), 0.0), btp, u0, ktp, v)
        for u in range(UNROLL):
            rh_ref[rows[u], :] = rh[u]
            y0_ref[rows[u], :] = y0[u]
            g_ref[cs[u], 0] = gm[u]
            h_ref[cs[u], 0] = hm[u]
        return carry

    lax.fori_loop(0, CB // UNROLL, body, 0)


def _scan_a(rt, at, bt, kt, v, pc):
    L, D = rt.shape
    NP = D // LANES
    NC = L // CHUNK
    CB = min(16, NC)
    blk = pl.BlockSpec((CB * CHUNK, LANES), lambda ci, p: (ci, p))
    gh = pl.BlockSpec((CB, 1, LANES, LANES), lambda ci, p: (ci, p, 0, 0))
    return pl.pallas_call(
        functools.partial(_scan_a_kernel, CB=CB, UNROLL=min(16, CB)),
        grid=(NC // CB, NP),
        in_specs=[blk] * 5 + [pl.BlockSpec((CB, 1, LANES), lambda ci, p: (ci, 0, p))],
        out_specs=[blk, blk, gh, gh],
        out_shape=[jax.ShapeDtypeStruct((L, D), F32), jax.ShapeDtypeStruct((L, D), F32),
                   jax.ShapeDtypeStruct((NC, NP, LANES, LANES), F32),
                   jax.ShapeDtypeStruct((NC, NP, LANES, LANES), F32)],
        compiler_params=_params(("parallel", "parallel")),
        name="rwkv_chunk_local",
    )(rt, at, bt, kt, v, pc)


def _scan_b_kernel(rh_ref, y0_ref, g_ref, h_ref, bonus_ref, gate_ref, lg_ref, lb_ref, o_ref,
                   st_ref, *, CB, PP):
    ci = pl.program_id(1)

    @pl.when(ci == 0)
    def _():
        st_ref[...] = jnp.zeros(st_ref.shape, F32)

    li = lax.broadcasted_iota(jnp.int32, (LANES, LANES), 0) // RWKV_HEAD
    lj = lax.broadcasted_iota(jnp.int32, (LANES, LANES), 1) // RWKV_HEAD
    e_mean = jnp.where(li == lj, 1.0 / RWKV_HEAD, 0.0).astype(BF16)

    def body(c, carry):
        rows = pl.ds(pl.multiple_of(c * CHUNK, CHUNK), CHUNK)
        sls = [slice(p * LANES, (p + 1) * LANES) for p in range(PP)]

        def each(fn, *cols):
            return [fn(*args) for args in zip(*cols)]

        st = [st_ref[p] for p in range(PP)]
        y = each(lambda sl, s: _dot3(rh_ref[rows, sl], s) + y0_ref[rows, sl], sls, st)
        st_new = each(lambda p, s: _dot3(g_ref[c, p], s) + h_ref[c, p], list(range(PP)), st)
        for p in range(PP):
            st_ref[p] = st_new[p]
        mean = each(lambda x: _dot_exact_rhs(x, e_mean), y)
        d = each(lambda x, m: x - m, y, mean)
        var = each(lambda x: _dot_exact_rhs(x * x, e_mean), d)
        for p in range(PP):
            sl = sls[p]
            yn = d[p] * lax.rsqrt(var[p] + RWKV_GN_EPS) * lg_ref[:, sl] + lb_ref[:, sl]
            o_ref[rows, sl] = ((yn + bonus_ref[rows, sl]) * gate_ref[rows, sl]).astype(o_ref.dtype)
        return carry

    lax.fori_loop(0, CB, body, 0)


def _scan_b(rh, y0, g, h, bonus, gate, ln_g, ln_b):
    L, D = rh.shape
    NP = D // LANES
    NC = L // CHUNK
    CB = min(4, NC)
    PP = 16
    blk = pl.BlockSpec((CB * CHUNK, PP * LANES), lambda pg, ci: (ci, pg))
    gh = pl.BlockSpec((CB, PP, LANES, LANES), lambda pg, ci: (ci, pg, 0, 0))
    vec = pl.BlockSpec((1, PP * LANES), lambda pg, ci: (0, pg))
    return pl.pallas_call(
        functools.partial(_scan_b_kernel, CB=CB, PP=PP),
        grid=(NP // PP, NC // CB),
        in_specs=[blk, blk, gh, gh, blk, blk, vec, vec],
        out_specs=blk,
        out_shape=jax.ShapeDtypeStruct((L, D), BF16),
        scratch_shapes=[pltpu.VMEM((PP, LANES, LANES), F32)],
        compiler_params=_params(("parallel", "arbitrary")),
        name="rwkv_state_scan",
    )(rh, y0, g, h, bonus, gate, ln_g.reshape(1, -1), ln_b.reshape(1, -1))


def _ct_kernel(w_ref, o_ref):
    o_ref[...] = w_ref[0].T.astype(BF16)


def _ct_patch_kernel(w_ref, patch_ref, o_ref, *, first_patch_row):
    w = w_ref[0]
    rows = lax.broadcasted_iota(jnp.int32, w.shape, 0)
    last = pl.program_id(0) == pl.num_programs(0) - 1
    w = jnp.where(last & (rows >= first_patch_row), patch_ref[...], w)
    o_ref[...] = w.T.astype(BF16)


def _cast_transpose(w_t, l, n_cols, tn, row_of_tile, name, patch=None, first_patch_row=0):
    K = w_t.shape[2]
    in_specs = [pl.BlockSpec((pl.Element(1), pl.Element(tn), pl.Element(K)),
                             lambda j: (l, pl.multiple_of(row_of_tile(j), 16), 0))]
    args = [w_t]
    body = _ct_kernel
    if patch is not None:
        in_specs.append(pl.BlockSpec((tn, K), lambda j: (0, 0)))
        args.append(patch)
        body = functools.partial(_ct_patch_kernel, first_patch_row=first_patch_row)
    return pl.pallas_call(
        body,
        grid=(n_cols // tn,),
        in_specs=in_specs,
        out_specs=pl.BlockSpec((K, tn), lambda j: (0, j)),
        out_shape=jax.ShapeDtypeStruct((K, n_cols), BF16),
        compiler_params=_params(("parallel",)),
        name=name,
    )(*args)


def _pack_in_proj(w_t, l, vdown):
    tn = 512
    n_plain = IN_IDX // tn
    w_b = _cast_transpose(
        w_t, l, ZB_WIDTH, tn,
        lambda j: jnp.where(j < n_plain, j * tn, IN_GATE + (j - n_plain) * tn), "pack_w_b")
    n_own = ZR_XV - (ZR_WIDTH - tn)
    patch = jnp.pad(vdown.T, ((n_own, 0), (0, 0)))
    w_r = _cast_transpose(w_t, l, ZR_WIDTH, tn, lambda j: IN_RWKV + j * tn, "pack_w_r",
                          patch=patch, first_patch_row=n_own)
    w_i = _cast_transpose(w_t, l, ZI_WIDTH, 256, lambda j: IN_IDX + j * 256, "pack_w_i")
    return w_b, w_r, w_i


def _rows_at(w, start, height):
    return jnp.pad(w, ((start, height - start - w.shape[0]), (0, 0)))


def kernel(x, w_in, norm_mix, dw_weight, dw_bias, conv_ln_g, conv_ln_b, w_conv_out, w_att_out, rwkv_mu, rwkv_w0, rwkv_w2, rwkv_a0, rwkv_a2, rwkv_g2, rwkv_k_k, rwkv_k_a, rwkv_r_k, rwkv_ln_g, rwkv_ln_b, vres_down, vres_mu, vres_up, vres_bias, w_rwkv_out, w_out, norm_ffn, w_ffn_gate, w_ffn_up, w_ffn_down, norm_final):
    B, L, D = x.shape
    assert B == 1 and D == D_MODEL and L % 256 == 0
    depth = w_in.shape[0]
    xs = x.reshape(L, D)
    assert w_in.shape[1:] == (D_MODEL, D_IN)
    w_t = jnp.swapaxes(w_in, 1, 2)
    wc_b, wa_b, wr_b, wo_b = (w.astype(BF16) for w in (w_conv_out, w_att_out, w_rwkv_out, w_out))
    wg_b, wu_b, wd_b = (w.astype(BF16) for w in (w_ffn_gate, w_ffn_up, w_ffn_down))
    v_first = None
    for l in range(depth):
        vdown = vres_down[l - 1] if l > 0 else jnp.zeros((D, LORA_MV), F32)
        w_b, w_r, w_i = _pack_in_proj(w_t, l, vdown)
        mu_p = jnp.concatenate([rwkv_mu[l], vres_mu[l - 1] if l > 0 else jnp.zeros((LORA_MV,), F32)])
        w2_p = _rows_at(rwkv_w2[l], 0, LANES).astype(BF16)
        a2_p = _rows_at(rwkv_a2[l], ZR_XA - ZR_XA_WIN[0], ZR_XA_WIN[1] - ZR_XA_WIN[0]).astype(BF16)
        g2_p = _rows_at(rwkv_g2[l], ZR_XG - ZR_XG_WIN[0], ZR_XG_WIN[1] - ZR_XG_WIN[0]).astype(BF16)

        h = _rms_norm(xs, norm_mix[l], BF16)
        zb = _matmul(h, w_b, BF16, 1024, 512, "in_proj_bf16")
        zr = _matmul(h, w_r, F32, 1024, 512, "in_proj_rwkv")
        zi = _matmul(h, w_i, F32, 1024, 256, "in_proj_index")

        a_mix = _conformer_conv(zb, dw_weight[l], dw_bias[l], conv_ln_g[l], conv_ln_b[l])

        qr, kr, iqr, ik2, iws = _att_prep(zb, zi)
        b_mix = _dsa_attention(qr, iqr, iws, ik2, kr, zb)

        vres = None
        if l > 0:
            vup_p = _rows_at(vres_up[l - 1], ZR_XV - ZR_XV_WIN[0], LANES).astype(BF16)
            vres = (v_first, vup_p, vres_bias[l - 1])
        rt, at, bt, kt, v_rwkv, bonus, gate, pc = _rwkv_prep(
            zr, mu_p, rwkv_w0[l], w2_p, rwkv_a0[l], a2_p, g2_p,
            rwkv_k_k[l], rwkv_k_a[l], rwkv_r_k[l].reshape(-1), vres)
        if l == 0:
            v_first = v_rwkv
        rh, y0, g_mat, h_mat = _scan_a(rt, at, bt, kt, v_rwkv, pc)
        c_mix = _scan_b(rh, y0, g_mat, h_mat, bonus, gate, rwkv_ln_g[l], rwkv_ln_b[l])

        merged = _merge(a_mix, b_mix, c_mix, wc_b, wa_b, wr_b, l, zb, 1024, 512)
        xs = _matmul_residual(merged, wo_b, l, xs, 1024, 512, "out_proj")

        h2 = _rms_norm(xs, norm_ffn[l], BF16)
        act = _ffn1(h2, wg_b, wu_b, l, 1024, 256)
        xs = _matmul_residual(act, wd_b, l, xs, 512, 256, "ffn_down")
    return _rms_norm(xs, norm_final, F32).reshape(B, L, D)
```

```python
import functools

import jax
import jax.numpy as jnp
import numpy as np
from jax import lax
from jax.experimental import pallas as pl
from jax.experimental.pallas import tpu as pltpu

F32 = jnp.float32
BF16 = jnp.bfloat16

D_MODEL = 4096
RMS_EPS = 1e-6
LN_EPS = 1e-5
D_CONV = D_MODEL // 4
CONV_WIDTH = 31
ATT_HEADS = 8
ATT_HEAD_DIM = 128
D_ATT = ATT_HEADS * ATT_HEAD_DIM
ATT_SCALE = ATT_HEAD_DIM ** -0.5
LOG2_E = 1.4426950408889634
IDX_HEADS = 16
IDX_HEAD_DIM = 64
IDX_SCALE = IDX_HEAD_DIM ** -0.5
IDX_W_SCALE = IDX_HEADS ** -0.5
TOPK_MAX = 256
ROPE_THETA = 500000.0
ROPE_FRACTION = 4
D_RWKV = D_MODEL // 2
RWKV_HEAD = 64
LORA_DECAY = 96
LORA_AAA = 96
LORA_MV = 64
LORA_GATE = 256
RWKV_GN_EPS = 64e-5
D_FF = ((8 * D_MODEL + 3 * 256 - 1) // (3 * 256)) * 256

LANES = 128
CHUNK = 64
IN_ATT = 2 * D_CONV
IN_IDX = IN_ATT + 3 * D_ATT
IN_RWKV = IN_IDX + IDX_HEADS * IDX_HEAD_DIM + IDX_HEAD_DIM + IDX_HEADS
IN_GATE = IN_RWKV + 3 * D_RWKV + LORA_DECAY + LORA_AAA + LORA_GATE
D_IN = IN_GATE + 3 * D_MODEL
ZR_XW = 3 * D_RWKV
ZR_XA = ZR_XW + LORA_DECAY
ZR_XG = ZR_XA + LORA_AAA
ZR_XV = ZR_XG + LORA_GATE
ZR_WIDTH = ZR_XV + LORA_MV
ZR_XA_WIN = (ZR_XW, ZR_XW + 2 * LANES)
ZR_XG_WIN = (ZR_XW + LANES, ZR_WIDTH)
ZR_XV_WIN = (ZR_WIDTH - LANES, ZR_WIDTH)
ZI_IK = IDX_HEADS * IDX_HEAD_DIM
ZI_IW_LANE = IDX_HEAD_DIM
ZI_WIDTH = ZI_IK + 2 * LANES
ZB_GATE = 2 * D_CONV + 3 * D_ATT
ZB_WIDTH = ZB_GATE + 3 * D_MODEL
VMEM_LIMIT = 56 * 1024 * 1024
INT_MIN = -2147483648
NEG_BIG = -1e30

NT_DIMS = (((1,), (1,)), ((), ()))
TN_DIMS = (((0,), (0,)), ((), ()))


def _params(sem, vmem=VMEM_LIMIT):
    return pltpu.CompilerParams(dimension_semantics=sem, vmem_limit_bytes=vmem)


def _dot(a, b):
    return jnp.dot(a, b, preferred_element_type=F32)


def _split2(x):
    hi = x.astype(BF16)
    lo = (x - hi.astype(F32)).astype(BF16)
    return hi, lo


def _dot_exact_rhs(x, e):
    h1 = x.astype(BF16)
    r1 = x - h1.astype(F32)
    h2 = r1.astype(BF16)
    h3 = (r1 - h2.astype(F32)).astype(BF16)
    return _dot(h1, e) + _dot(h2, e) + _dot(h3, e)


def _dot_exact_lhs(e, x):
    h1 = x.astype(BF16)
    r1 = x - h1.astype(F32)
    h2 = r1.astype(BF16)
    h3 = (r1 - h2.astype(F32)).astype(BF16)
    return _dot(e, h1) + _dot(e, h2) + _dot(e, h3)


def _dot3(a, b):
    ah, al = _split2(a)
    bh, bl = _split2(b)
    return _dot(ah, bh) + _dot(ah, bl) + _dot(al, bh)


def _sigmoid(x):
    return 1.0 / (1.0 + jnp.exp(-x))


def _rms_kernel(x_ref, g_ref, o_ref):
    x = x_ref[...]
    y = x * lax.rsqrt(jnp.mean(x * x, axis=-1, keepdims=True) + RMS_EPS)
    o_ref[...] = (y * g_ref[...]).astype(o_ref.dtype)


def _rms_norm(x, g, out_dtype):
    L, D = x.shape
    tr = min(256, L)
    return pl.pallas_call(
        _rms_kernel,
        grid=(L // tr,),
        in_specs=[pl.BlockSpec((tr, D), lambda i: (i, 0)),
                  pl.BlockSpec((1, D), lambda i: (0, 0))],
        out_specs=pl.BlockSpec((tr, D), lambda i: (i, 0)),
        out_shape=jax.ShapeDtypeStruct((L, D), out_dtype),
        compiler_params=_params(("parallel",)),
        name="rms_norm",
    )(x, g.reshape(1, D))


def _mm_kernel(a_ref, b_ref, o_ref):
    o_ref[...] = _dot(a_ref[...], b_ref[...]).astype(o_ref.dtype)


def _matmul(a, b, out_dtype, tm, tn, name):
    M, K = a.shape
    N = b.shape[1]
    tm = min(tm, M)
    return pl.pallas_call(
        _mm_kernel,
        grid=(M // tm, N // tn),
        in_specs=[pl.BlockSpec((tm, K), lambda i, j: (i, 0)),
                  pl.BlockSpec((K, tn), lambda i, j: (0, j))],
        out_specs=pl.BlockSpec((tm, tn), lambda i, j: (i, j)),
        out_shape=jax.ShapeDtypeStruct((M, N), out_dtype),
        compiler_params=_params(("parallel", "parallel")),
        name=name,
    )(a, b)


def _mm_res_kernel(a_ref, b_ref, x_ref, o_ref):
    o_ref[...] = x_ref[...] + _dot(a_ref[...], b_ref[...])


def _layer_weight_spec(w, l, tn):
    return pl.BlockSpec((None, w.shape[1], tn), lambda i, j: (l, 0, j))


def _matmul_residual(a, b, l, x, tm, tn, name):
    M, K = a.shape
    N = b.shape[2]
    tm = min(tm, M)
    return pl.pallas_call(
        _mm_res_kernel,
        grid=(M // tm, N // tn),
        in_specs=[pl.BlockSpec((tm, K), lambda i, j: (i, 0)),
                  _layer_weight_spec(b, l, tn),
                  pl.BlockSpec((tm, tn), lambda i, j: (i, j))],
        out_specs=pl.BlockSpec((tm, tn), lambda i, j: (i, j)),
        out_shape=jax.ShapeDtypeStruct((M, N), F32),
        compiler_params=_params(("parallel", "parallel")),
        name=name,
    )(a, b, x)


def _ffn1_kernel(h_ref, wg_ref, wu_ref, o_ref):
    h = h_ref[...]
    g = _dot(h, wg_ref[...])
    u = _dot(h, wu_ref[...])
    o_ref[...] = (g * _sigmoid(g) * u).astype(o_ref.dtype)


def _ffn1(h, wg, wu, l, tm, tn):
    M, K = h.shape
    N = wg.shape[2]
    tm = min(tm, M)
    return pl.pallas_call(
        _ffn1_kernel,
        grid=(M // tm, N // tn),
        in_specs=[pl.BlockSpec((tm, K), lambda i, j: (i, 0)),
                  _layer_weight_spec(wg, l, tn),
                  _layer_weight_spec(wu, l, tn)],
        out_specs=pl.BlockSpec((tm, tn), lambda i, j: (i, j)),
        out_shape=jax.ShapeDtypeStruct((M, N), BF16),
        compiler_params=_params(("parallel", "parallel")),
        name="ffn_gate_up",
    )(h, wg, wu)


def _merge_kernel(a_ref, b_ref, c_ref, wa_ref, wb_ref, wc_ref, g0_ref, g1_ref, g2_ref, o_ref):
    ya = _dot(a_ref[...], wa_ref[...])
    yb = _dot(b_ref[...], wb_ref[...])
    yc = _dot(c_ref[...], wc_ref[...])
    m = (_sigmoid(g0_ref[...].astype(F32)) * ya + _sigmoid(g1_ref[...].astype(F32)) * yb
         + _sigmoid(g2_ref[...].astype(F32)) * yc)
    o_ref[...] = m.astype(o_ref.dtype)


def _merge(a, b, c, wa, wb, wc, l, zb, tm, tn):
    M = a.shape[0]
    tm = min(tm, M)
    g_off = ZB_GATE // tn
    g_stride = D_MODEL // tn
    return pl.pallas_call(
        _merge_kernel,
        grid=(M // tm, D_MODEL // tn),
        in_specs=[pl.BlockSpec((tm, D_CONV), lambda i, j: (i, 0)),
                  pl.BlockSpec((tm, D_ATT), lambda i, j: (i, 0)),
                  pl.BlockSpec((tm, D_RWKV), lambda i, j: (i, 0)),
                  _layer_weight_spec(wa, l, tn),
                  _layer_weight_spec(wb, l, tn),
                  _layer_weight_spec(wc, l, tn),
                  pl.BlockSpec((tm, tn), lambda i, j: (i, g_off + j)),
                  pl.BlockSpec((tm, tn), lambda i, j: (i, g_off + g_stride + j)),
                  pl.BlockSpec((tm, tn), lambda i, j: (i, g_off + 2 * g_stride + j))],
        out_specs=pl.BlockSpec((tm, tn), lambda i, j: (i, j)),
        out_shape=jax.ShapeDtypeStruct((M, D_MODEL), BF16),
        compiler_params=_params(("parallel", "parallel")),
        name="gated_merge",
    )(a, b, c, wa, wb, wc, zb, zb, zb)


def _conv_kernel(val_ref, gate_ref, hval_ref, hgate_ref, w_ref, b_ref, lg_ref, lb_ref, o_ref,
                 c_ref, *, T):
    i = pl.program_id(0)
    halo = 32
    hv = hval_ref[...].astype(F32)
    hg = hgate_ref[...].astype(F32)
    hc = hv * _sigmoid(hg)
    c_ref[0:halo, :] = jnp.where(i > 0, hc, 0.0)
    v = val_ref[...].astype(F32)
    g = gate_ref[...].astype(F32)
    c_ref[halo:halo + T, :] = v * _sigmoid(g)
    acc = jnp.zeros((T, D_CONV), F32) + b_ref[...]
    for j in range(CONV_WIDTH):
        off = halo - (CONV_WIDTH - 1) + j
        acc = acc + c_ref[off:off + T, :] * w_ref[j:j + 1, :]
    m = jnp.mean(acc, axis=-1, keepdims=True)
    d = acc - m
    var = jnp.mean(d * d, axis=-1, keepdims=True)
    y = d * lax.rsqrt(var + LN_EPS) * lg_ref[...] + lb_ref[...]
    o_ref[...] = (y * _sigmoid(y)).astype(o_ref.dtype)


def _conformer_conv(zb, dw_w, dw_b, ln_g, ln_b):
    L = zb.shape[0]
    T = min(256, L)
    hb = T // 32
    w_pad = jnp.zeros((32, D_CONV), F32).at[:CONV_WIDTH].set(dw_w)
    return pl.pallas_call(
        functools.partial(_conv_kernel, T=T),
        grid=(L // T,),
        in_specs=[pl.BlockSpec((T, D_CONV), lambda i: (i, 0)),
                  pl.BlockSpec((T, D_CONV), lambda i: (i, 1)),
                  pl.BlockSpec((32, D_CONV), lambda i: (jnp.maximum(i * hb - 1, 0), 0)),
                  pl.BlockSpec((32, D_CONV), lambda i: (jnp.maximum(i * hb - 1, 0), 1)),
                  pl.BlockSpec((32, D_CONV), lambda i: (0, 0)),
                  pl.BlockSpec((1, D_CONV), lambda i: (0, 0)),
                  pl.BlockSpec((1, D_CONV), lambda i: (0, 0)),
                  pl.BlockSpec((1, D_CONV), lambda i: (0, 0))],
        out_specs=pl.BlockSpec((T, D_CONV), lambda i: (i, 0)),
        out_shape=jax.ShapeDtypeStruct((L, D_CONV), BF16),
        scratch_shapes=[pltpu.VMEM((T + 32, D_CONV), F32)],
        compiler_params=_params(("parallel",)),
        name="conformer_conv",
    )(zb, zb, zb, zb, w_pad, dw_b.reshape(1, -1), ln_g.reshape(1, -1), ln_b.reshape(1, -1))


def _rope_tables(pos, inv_freq, head_dim):
    rd = head_dim // ROPE_FRACTION
    half = rd // 2
    ang = pos * inv_freq
    cos = jnp.cos(ang)
    sin = jnp.sin(ang)
    lane = lax.broadcasted_iota(jnp.int32, ang.shape, 1) % head_dim
    c = jnp.where(lane < rd, cos, 1.0)
    s_up = jnp.where(lane < half, -sin, 0.0)
    s_dn = jnp.where((lane >= half) & (lane < rd), sin, 0.0)
    return c, s_up, s_dn


def _rope_apply(x, tabs, half):
    c, s_up, s_dn = tabs
    return (x * c + pltpu.roll(x, LANES - half, 1) * s_up + pltpu.roll(x, half, 1) * s_dn)


def _attprep_kernel(q_ref, k_ref, zi_ref, fq_ref, fi_ref, qo_ref, ko_ref, iqo_ref, iko_ref,
                    iwo_ref, *, T):
    i = pl.program_id(0)
    pos = (i * T + lax.broadcasted_iota(jnp.int32, (T, LANES), 0)).astype(F32)
    tq = _rope_tables(pos, fq_ref[...], ATT_HEAD_DIM)
    ti = _rope_tables(pos, fi_ref[...], IDX_HEAD_DIM)
    hq = ATT_HEAD_DIM // ROPE_FRACTION // 2
    hi = IDX_HEAD_DIM // ROPE_FRACTION // 2
    for h in range(ATT_HEADS):
        sl = slice(h * LANES, (h + 1) * LANES)
        qo_ref[:, sl] = (_rope_apply(q_ref[:, sl].astype(F32), tq, hq)
                         * (ATT_SCALE * LOG2_E)).astype(BF16)
        ko_ref[:, sl] = _rope_apply(k_ref[:, sl].astype(F32), tq, hq).astype(BF16)
    for p in range(ZI_IK // LANES):
        sl = slice(p * LANES, (p + 1) * LANES)
        iqo_ref[:, sl] = _rope_apply(zi_ref[:, sl], ti, hi).astype(BF16)
    grp = zi_ref[:, ZI_IK:ZI_IK + LANES]
    ik = _rope_apply(grp, ti, hi)
    lane = lax.broadcasted_iota(jnp.int32, (T, LANES), 1)
    iko_ref[...] = jnp.where(lane < IDX_HEAD_DIM, ik, pltpu.roll(ik, IDX_HEAD_DIM, 1)).astype(BF16)
    iwo_ref[...] = grp * (IDX_W_SCALE * IDX_SCALE)


def _lane_inv_freq(head_dim):
    rd = head_dim // ROPE_FRACTION
    half = rd // 2
    inv_freq = jnp.power(ROPE_THETA, -jnp.arange(half, dtype=F32) * (2.0 / rd))
    lane = np.arange(LANES) % head_dim
    idx = np.where(lane < rd, lane % half, 0)
    return jnp.where(jnp.asarray(lane < rd), inv_freq[idx], 0.0).reshape(1, LANES)


def _att_prep(zb, zi):
    L = zb.shape[0]
    T = min(256, L)
    row = lambda i: (i, 0)
    return pl.pallas_call(
        functools.partial(_attprep_kernel, T=T),
        grid=(L // T,),
        in_specs=[pl.BlockSpec((T, D_ATT), lambda i: (i, 2)),
                  pl.BlockSpec((T, D_ATT), lambda i: (i, 3)),
                  pl.BlockSpec((T, ZI_WIDTH), row),
                  pl.BlockSpec((1, LANES), lambda i: (0, 0)),
                  pl.BlockSpec((1, LANES), lambda i: (0, 0))],
        out_specs=[pl.BlockSpec((T, D_ATT), row), pl.BlockSpec((T, D_ATT), row),
                   pl.BlockSpec((T, ZI_IK), row), pl.BlockSpec((T, LANES), row),
                   pl.BlockSpec((T, LANES), row)],
        out_shape=[jax.ShapeDtypeStruct((L, D_ATT), BF16), jax.ShapeDtypeStruct((L, D_ATT), BF16),
                   jax.ShapeDtypeStruct((L, ZI_IK), BF16), jax.ShapeDtypeStruct((L, LANES), BF16),
                   jax.ShapeDtypeStruct((L, LANES), F32)],
        compiler_params=_params(("parallel",)),
        name="att_prep",
    )(zb, zb, zi, _lane_inv_freq(ATT_HEAD_DIM), _lane_inv_freq(IDX_HEAD_DIM))


def _dsa_kernel(q_ref, iq_ref, iw_ref, ik_ref, k_ref, v_ref, o_ref,
                keyt_ref, wt_ref, iqm_ref, thr_ref, cst_ref, m_ref, l_ref, acc_ref,
                *, Q, S, SLAB, topk):
    i = pl.program_id(0)
    j = pl.program_id(1)
    q_end = (i + 1) * Q

    @pl.when(j == 0)
    def _select():
        lane = lax.broadcasted_iota(jnp.int32, (Q, LANES), 1)
        lo = lane < IDX_HEAD_DIM
        zero = jnp.zeros((Q, LANES), BF16)
        for p in range(IDX_HEADS // 2):
            x = iq_ref[:, p * LANES:(p + 1) * LANES]
            iqm_ref[2 * p] = jnp.where(lo, x, zero)
            iqm_ref[2 * p + 1] = jnp.where(lo, zero, x)
        wt_ref[...] = iw_ref[...].T
        qpos = i * Q + lax.broadcasted_iota(jnp.int32, (S, Q), 1)

        def score_chunk(c, carry):
            off = pl.multiple_of(c * S, S)
            ikc = ik_ref[pl.ds(off, S), :]
            sc = jnp.zeros((S, Q), F32)
            for h in range(IDX_HEADS):
                lg = lax.dot_general(ikc, iqm_ref[h], NT_DIMS, preferred_element_type=F32)
                sc = sc + jnp.maximum(lg, 0.0) * wt_ref[ZI_IW_LANE + h:ZI_IW_LANE + h + 1, :]
            sc = jnp.where(sc == 0.0, 0.0, sc)
            bits = pltpu.bitcast(sc, jnp.int32)
            key = bits ^ ((bits >> 31) & 0x7FFFFFFF)
            kpos = off + lax.broadcasted_iota(jnp.int32, (S, Q), 0)
            keyt_ref[pl.ds(off, S), :] = jnp.where(kpos <= qpos, key, INT_MIN)
            return carry

        nchunk = (q_end + S - 1) // S
        lax.fori_loop(0, nchunk, score_chunk, 0)

        def count_keys(one_if_hit):
            def body(c, cnt):
                off = pl.multiple_of(c * S, S)
                for u in range(S // SLAB):
                    kc = keyt_ref[pl.ds(off + u * SLAB, SLAB), :]
                    cnt = cnt + one_if_hit(kc, off + u * SLAB)
                return cnt
            cnt = lax.fori_loop(0, nchunk, body, jnp.zeros((SLAB, Q), F32))
            return jnp.sum(cnt, axis=0, keepdims=True)

        def bit_step(b, carry):
            prefix, n_ge = carry
            cand_u = prefix | jnp.left_shift(jnp.int32(1), 31 - b)
            cand = jnp.broadcast_to(cand_u ^ INT_MIN, (SLAB, Q))
            tot = count_keys(lambda kc, _: jnp.where(kc >= cand, 1.0, 0.0))
            ok = tot >= float(topk)
            return jnp.where(ok, cand_u, prefix), jnp.where(ok, tot, n_ge)

        prefix, n_ge = lax.fori_loop(
            0, 32, bit_step, (jnp.zeros((1, Q), jnp.int32), jnp.zeros((1, Q), F32)))
        thr = jnp.maximum(prefix ^ INT_MIN, INT_MIN + 1)
        thr_ref[...] = jnp.broadcast_to(thr, thr_ref.shape)
        cst_ref[...] = jnp.full(cst_ref.shape, 2147483647, jnp.int32)

        @pl.when(jnp.max(n_ge) > float(topk))
        def _break_ties():
            thr_b = jnp.broadcast_to(thr, (SLAB, Q))
            need = float(topk) - count_keys(lambda kc, _: jnp.where(kc > thr_b, 1.0, 0.0))
            slab_pos = lax.broadcasted_iota(jnp.int32, (SLAB, Q), 0)
            nbits = max(1, (keyt_ref.shape[0] - 1).bit_length())

            def pos_step(b, ans):
                t = ans | jnp.left_shift(jnp.int32(1), nbits - 1 - b)
                lim = jnp.broadcast_to(t - 1, (SLAB, Q))
                below = count_keys(
                    lambda kc, p0: jnp.where(kc == thr_b,
                                             jnp.where(slab_pos + p0 <= lim, 1.0, 0.0), 0.0))
                return jnp.where(below < need, t, ans)

            cst = lax.fori_loop(0, nbits, pos_step, jnp.zeros((1, Q), jnp.int32))
            cst_ref[...] = jnp.broadcast_to(cst, cst_ref.shape)

        m_ref[...] = jnp.full(m_ref.shape, NEG_BIG, F32)
        l_ref[...] = jnp.zeros(l_ref.shape, F32)
        acc_ref[...] = jnp.zeros(acc_ref.shape, F32)

    @pl.when(j * S < q_end)
    def _attend():
        keys = keyt_ref[pl.ds(pl.multiple_of(j * S, S), S), :]
        kpos = j * S + lax.broadcasted_iota(jnp.int32, (S, Q), 0)
        thr = thr_ref[0:1, :]
        at_thr = jnp.where(kpos <= cst_ref[0:1, :], 0.0, NEG_BIG)
        bias = jnp.where(keys > thr, 0.0, jnp.where(keys == thr, at_thr, NEG_BIG)).T
        for h in range(ATT_HEADS):
            sl = slice(h * LANES, (h + 1) * LANES)
            s = lax.dot_general(q_ref[:, sl], k_ref[:, sl], NT_DIMS, preferred_element_type=F32)
            s = s + bias
            m_prev = m_ref[h]
            m_new = jnp.maximum(m_prev, jnp.max(s, axis=1, keepdims=True))
            alpha = jnp.exp2(m_prev - m_new)
            p = jnp.exp2(s - jnp.tile(m_new, (1, S // LANES)))
            l_ref[h] = alpha * l_ref[h] + jnp.sum(p, axis=1, keepdims=True)
            acc_ref[:, sl] = alpha * acc_ref[:, sl] + _dot(p.astype(BF16), v_ref[:, sl])
            m_ref[h] = m_new

    @pl.when(j == pl.num_programs(1) - 1)
    def _finish():
        for h in range(ATT_HEADS):
            sl = slice(h * LANES, (h + 1) * LANES)
            o_ref[:, sl] = (acc_ref[:, sl] / l_ref[h]).astype(o_ref.dtype)


def _dsa_attention(qr, iqr, iws, ik2, kr, zb):
    L = qr.shape[0]
    Q = min(512, L)
    S = min(512, L)
    SLAB = 32
    topk = min(TOPK_MAX, L // 4)
    nk = L // S

    def kv_block(i, j):
        return jnp.minimum(j, ((i + 1) * Q - 1) // S)

    return pl.pallas_call(
        functools.partial(_dsa_kernel, Q=Q, S=S, SLAB=SLAB, topk=topk),
        grid=(L // Q, nk),
        in_specs=[pl.BlockSpec((Q, D_ATT), lambda i, j: (i, 0)),
                  pl.BlockSpec((Q, ZI_IK), lambda i, j: (i, 0)),
                  pl.BlockSpec((Q, LANES), lambda i, j: (i, 0)),
                  pl.BlockSpec((L, LANES), lambda i, j: (0, 0)),
                  pl.BlockSpec((S, D_ATT), lambda i, j: (kv_block(i, j), 0)),
                  pl.BlockSpec((S, D_ATT), lambda i, j: (kv_block(i, j), 4))],
        out_specs=pl.BlockSpec((Q, D_ATT), lambda i, j: (i, 0)),
        out_shape=jax.ShapeDtypeStruct((L, D_ATT), BF16),
        scratch_shapes=[pltpu.VMEM((L, Q), jnp.int32),
                        pltpu.VMEM((LANES, Q), F32),
                        pltpu.VMEM((IDX_HEADS, Q, LANES), BF16),
                        pltpu.VMEM((8, Q), jnp.int32),
                        pltpu.VMEM((8, Q), jnp.int32),
                        pltpu.VMEM((ATT_HEADS, Q, LANES), F32),
                        pltpu.VMEM((ATT_HEADS, Q, LANES), F32),
                        pltpu.VMEM((Q, D_ATT), F32)],
        compiler_params=_params(("parallel", "arbitrary")),
        name="dsa_attention",
    )(qr, iqr, iws, ik2, kr, zb)


def _head_sum(x, e):
    cols = []
    for p in range(x.shape[1] // LANES):
        cols.append(_dot_exact_rhs(x[:, p * LANES:(p + 1) * LANES], e))
    return jnp.concatenate(cols, axis=1)


def _rwkv_prep_kernel(*refs, T, has_vres):
    if has_vres:
        (z_ref, halo_ref, mu_ref, w0_ref, w2_ref, a0_ref, a2_ref, g2_ref, kk_ref, ka_ref, rk_ref,
         vf_ref, vup_ref, vb_ref,
         rt_ref, at_ref, bt_ref, kt_ref, v_ref, bonus_ref, g_ref, pc_ref) = refs
    else:
        (z_ref, halo_ref, mu_ref, w0_ref, w2_ref, a0_ref, a2_ref, g2_ref, kk_ref, ka_ref, rk_ref,
         rt_ref, at_ref, bt_ref, kt_ref, v_ref, bonus_ref, g_ref, pc_ref) = refs
    i = pl.program_id(0)
    D = D_RWKV

    def shifted(lo, hi):
        z = z_ref[:, lo:hi]
        first = jnp.where(i > 0, halo_ref[7:8, lo:hi], 0.0)
        rows = lax.broadcasted_iota(jnp.int32, z.shape, 0)
        prev = jnp.where(rows == 0, first, pltpu.roll(z, 1, 0))
        return z + (prev - z) * mu_ref[:, lo:hi]

    r = shifted(0, D)
    kraw = shifted(D, 2 * D)
    v = shifted(2 * D, 3 * D)
    xw = shifted(ZR_XW, ZR_XW + LANES)
    xa = shifted(*ZR_XA_WIN)
    xg = shifted(*ZR_XG_WIN)
    if has_vres:
        xv = shifted(*ZR_XV_WIN)
        mix = _sigmoid(vb_ref[...] + _dot(xv.astype(BF16), vup_ref[...]))
        v = v + (vf_ref[...] - v) * mix
    v_ref[...] = v

    w_in = w0_ref[...] + _dot(jnp.tanh(xw).astype(BF16), w2_ref[...])
    w_log = -(jnp.maximum(-w_in, 0.0) + jnp.log(1.0 + jnp.exp(-jnp.abs(w_in)))) - 0.5
    logw = -jnp.exp(w_log)
    a_lr = _sigmoid(a0_ref[...] + _dot(xa.astype(BF16), a2_ref[...]))
    g_ref[...] = _dot(_sigmoid(xg).astype(BF16), g2_ref[...])

    li = lax.broadcasted_iota(jnp.int32, (LANES, LANES), 0) // RWKV_HEAD
    lj = lax.broadcasted_iota(jnp.int32, (LANES, LANES), 1) // RWKV_HEAD
    e_head = jnp.where(li == lj, 1.0, 0.0).astype(BF16)
    kk = kraw * kk_ref[...]
    norm = jnp.sqrt(_head_sum(kk * kk, e_head))
    kk = kk / jnp.maximum(norm, 1e-12)
    k = kraw * (1.0 + (a_lr - 1.0) * ka_ref[...])
    bonus_ref[...] = _head_sum(r * k * rk_ref[...], e_head) * v

    ti = lax.broadcasted_iota(jnp.int32, (T, T), 0)
    tj = lax.broadcasted_iota(jnp.int32, (T, T), 1)
    tri = jnp.where((ti // CHUNK == tj // CHUNK) & (tj <= ti), 1.0, 0.0).astype(BF16)
    cum = _dot_exact_lhs(tri, logw)
    p_in = jnp.exp(cum)
    p_out = jnp.exp(-cum)
    rt_ref[...] = r * p_in
    at_ref[...] = (-kk * jnp.exp(cum - logw)).astype(at_ref.dtype)
    bt_ref[...] = (kk * a_lr * p_out).astype(bt_ref.dtype)
    kt_ref[...] = (k * p_out).astype(kt_ref.dtype)
    for c in range(T // CHUNK):
        pc_ref[c] = p_in[(c + 1) * CHUNK - 1:(c + 1) * CHUNK, :]


def _rwkv_prep(zr, mu, w0, w2, a0, a2, g2, k_k, k_a, r_k, vres):
    L = zr.shape[0]
    T = min(128, L)
    D = D_RWKV
    row = lambda i: (i, 0)
    const = lambda i: (0, 0)
    vec = lambda a: a.reshape(1, -1)
    has_vres = vres is not None
    ins = [zr, zr, vec(mu), vec(w0), w2, vec(a0), a2, g2, vec(k_k), vec(k_a), vec(r_k)]
    in_specs = [pl.BlockSpec((T, ZR_WIDTH), row),
                pl.BlockSpec((8, ZR_WIDTH), lambda i: (jnp.maximum(i * (T // 8) - 1, 0), 0)),
                pl.BlockSpec((1, ZR_WIDTH), const),
                pl.BlockSpec((1, D), const), pl.BlockSpec(w2.shape, const),
                pl.BlockSpec((1, D), const), pl.BlockSpec(a2.shape, const),
                pl.BlockSpec(g2.shape, const),
                pl.BlockSpec((1, D), const), pl.BlockSpec((1, D), const), pl.BlockSpec((1, D), const)]
    if has_vres:
        v_first, v_up, v_bias = vres
        ins += [v_first, v_up, vec(v_bias)]
        in_specs += [pl.BlockSpec((T, D), row), pl.BlockSpec((LANES, D), const),
                     pl.BlockSpec((1, D), const)]
    big = jax.ShapeDtypeStruct((L, D), F32)
    return pl.pallas_call(
        functools.partial(_rwkv_prep_kernel, T=T, has_vres=has_vres),
        grid=(L // T,),
        in_specs=in_specs,
        out_specs=[pl.BlockSpec((T, D), row)] * 7
                  + [pl.BlockSpec((T // CHUNK, 1, D), lambda i: (i, 0, 0))],
        out_shape=[big] + [jax.ShapeDtypeStruct((L, D), BF16)] * 3 + [big] * 3
                  + [jax.ShapeDtypeStruct((L // CHUNK, 1, D), F32)],
        compiler_params=_params(("parallel",)),
        name="rwkv_prep",
    )(*ins)


def _stack2(x, lo):
    zero = jnp.zeros_like(x)
    return jnp.concatenate([jnp.where(lo, x, zero), jnp.where(lo, zero, x)], axis=0)


def _scan_a_kernel(rt_ref, at_ref, bt_ref, kt_ref, v_ref, pc_ref, rh_ref, y0_ref, g_ref, h_ref,
                   *, CB, UNROLL):
    C = CHUNK
    lane = lax.broadcasted_iota(jnp.int32, (C, LANES), 1)
    lo = lane < RWKV_HEAD
    spos = lane % RWKV_HEAD
    tpos = lax.broadcasted_iota(jnp.int32, (C, LANES), 0)
    strict = spos < tpos
    incl = spos <= tpos
    eye_c = jnp.where(spos == tpos, 1.0, 0.0)
    ri = lax.broadcasted_iota(jnp.int32, (LANES, LANES), 0)
    ci = lax.broadcasted_iota(jnp.int32, (LANES, LANES), 1)
    same_head = (ri // RWKV_HEAD) == (ci // RWKV_HEAD)
    diag = ri == ci

    def stk(x):
        return _stack2(x, lo).astype(BF16)

    nt = lambda a, b: lax.dot_general(a, b, NT_DIMS, preferred_element_type=F32)
    tn = lambda a, b: lax.dot_general(a, b, TN_DIMS, preferred_element_type=F32)

    def each(fn, *cols):
        return [fn(*args) for args in zip(*cols)]

    def body(cg, carry):
        cs = [cg * UNROLL + u for u in range(UNROLL)]
        rows = [pl.ds(pl.multiple_of(c * C, C), C) for c in cs]
        rt = [rt_ref[r, :] for r in rows]
        at = [at_ref[r, :] for r in rows]
        bt = [bt_ref[r, :] for r in rows]
        kt = [kt_ref[r, :] for r in rows]
        v = [v_ref[r, :] for r in rows]
        pc = [pc_ref[c] for c in cs]
        at_b = each(lambda x: x.astype(BF16), at)
        rt_b = each(lambda x: x.astype(BF16), rt)
        bs = each(stk, bt)
        ks = each(stk, kt)
        vs = each(stk, v)
        m_ab = each(lambda a, b: jnp.where(strict, nt(a, b), 0.0), at_b, bs)
        m_ak = each(lambda a, b: jnp.where(strict, nt(a, b), 0.0), at_b, ks)
        m_rb = each(lambda a, b: jnp.where(incl, nt(a, b), 0.0), rt_b, bs)
        m_rk = each(lambda a, b: jnp.where(incl, nt(a, b), 0.0), rt_b, ks)
        mv = each(lambda m, x: _dot(m.astype(BF16), x), m_ak, vs)
        pw = m_ab
        inv = each(lambda m: eye_c + m, pw)
        for _ in range(5):
            pw = each(lambda x: _dot(x.astype(BF16), stk(x)), pw)
            inv = each(lambda t, x: t + _dot(t.astype(BF16), stk(x)), inv, pw)
        inv_b = each(lambda x: x.astype(BF16), inv)
        w = each(lambda t, x: _dot(t, stk(x)), inv_b, at)
        u0 = each(lambda t, x: _dot(t, stk(x)), inv_b, mv)
        m_rb_b = each(lambda x: x.astype(BF16), m_rb)
        rh = each(lambda r, m, x: r + _dot(m, stk(x)), rt, m_rb_b, w)
        y0 = each(lambda m, x, m2, x2: _dot(m, stk(x)) + _dot(m2.astype(BF16), x2),
                  m_rb_b, u0, m_rk, vs)
        btp = each(lambda x, p: (x * p).astype(BF16), bt, pc)
        ktp = each(lambda x, p: (x * p).astype(BF16), kt, pc)
        gm = each(lambda b, x, p: jnp.where(diag, jnp.broadcast_to(p, (LANES, LANES)), 0.0)
                  + jnp.where(same_head, tn(b, x.astype(BF16)), 0.0), btp, w, pc)
        hm = each(lambda b, x, k, y: jnp.where(same_head, tn(b, x.astype(BF16))
                                               + tn(k, y.astype(BF16)), 0.0), btp, u0, ktp, v)
        for u in range(UNROLL):
            rh_ref[rows[u], :] = rh[u]
            y0_ref[rows[u], :] = y0[u]
            g_ref[cs[u], 0] = gm[u]
            h_ref[cs[u], 0] = hm[u]
        return carry

    lax.fori_loop(0, CB // UNROLL, body, 0)


def _scan_a(rt, at, bt, kt, v, pc):
    L, D = rt.shape
    NP = D // LANES
    NC = L // CHUNK
    CB = min(16, NC)
    blk = pl.BlockSpec((CB * CHUNK, LANES), lambda ci, p: (ci, p))
    gh = pl.BlockSpec((CB, 1, LANES, LANES), lambda ci, p: (ci, p, 0, 0))
    return pl.pallas_call(
        functools.partial(_scan_a_kernel, CB=CB, UNROLL=min(16, CB)),
        grid=(NC // CB, NP),
        in_specs=[blk] * 5 + [pl.BlockSpec((CB, 1, LANES), lambda ci, p: (ci, 0, p))],
        out_specs=[blk, blk, gh, gh],
        out_shape=[jax.ShapeDtypeStruct((L, D), F32), jax.ShapeDtypeStruct((L, D), F32),
                   jax.ShapeDtypeStruct((NC, NP, LANES, LANES), F32),
                   jax.ShapeDtypeStruct((NC, NP, LANES, LANES), F32)],
        compiler_params=_params(("parallel", "parallel")),
        name="rwkv_chunk_local",
    )(rt, at, bt, kt, v, pc)


def _scan_b_kernel(rh_ref, y0_ref, g_ref, h_ref, bonus_ref, gate_ref, lg_ref, lb_ref, o_ref,
                   st_ref, *, CB, PP):
    ci = pl.program_id(1)

    @pl.when(ci == 0)
    def _():
        st_ref[...] = jnp.zeros(st_ref.shape, F32)

    li = lax.broadcasted_iota(jnp.int32, (LANES, LANES), 0) // RWKV_HEAD
    lj = lax.broadcasted_iota(jnp.int32, (LANES, LANES), 1) // RWKV_HEAD
    e_mean = jnp.where(li == lj, 1.0 / RWKV_HEAD, 0.0).astype(BF16)

    def body(c, carry):
        rows = pl.ds(pl.multiple_of(c * CHUNK, CHUNK), CHUNK)
        sls = [slice(p * LANES, (p + 1) * LANES) for p in range(PP)]

        def each(fn, *cols):
            return [fn(*args) for args in zip(*cols)]

        st = [st_ref[p] for p in range(PP)]
        y = each(lambda sl, s: _dot3(rh_ref[rows, sl], s) + y0_ref[rows, sl], sls, st)
        st_new = each(lambda p, s: _dot3(g_ref[c, p], s) + h_ref[c, p], list(range(PP)), st)
        for p in range(PP):
            st_ref[p] = st_new[p]
        mean = each(lambda x: _dot_exact_rhs(x, e_mean), y)
        d = each(lambda x, m: x - m, y, mean)
        var = each(lambda x: _dot_exact_rhs(x * x, e_mean), d)
        for p in range(PP):
            sl = sls[p]
            yn = d[p] * lax.rsqrt(var[p] + RWKV_GN_EPS) * lg_ref[:, sl] + lb_ref[:, sl]
            o_ref[rows, sl] = ((yn + bonus_ref[rows, sl]) * gate_ref[rows, sl]).astype(o_ref.dtype)
        return carry

    lax.fori_loop(0, CB, body, 0)


def _scan_b(rh, y0, g, h, bonus, gate, ln_g, ln_b):
    L, D = rh.shape
    NP = D // LANES
    NC = L // CHUNK
    CB = min(4, NC)
    PP = 16
    blk = pl.BlockSpec((CB * CHUNK, PP * LANES), lambda pg, ci: (ci, pg))
    gh = pl.BlockSpec((CB, PP, LANES, LANES), lambda pg, ci: (ci, pg, 0, 0))
    vec = pl.BlockSpec((1, PP * LANES), lambda pg, ci: (0, pg))
    return pl.pallas_call(
        functools.partial(_scan_b_kernel, CB=CB, PP=PP),
        grid=(NP // PP, NC // CB),
        in_specs=[blk, blk, gh, gh, blk, blk, vec, vec],
        out_specs=blk,
        out_shape=jax.ShapeDtypeStruct((L, D), BF16),
        scratch_shapes=[pltpu.VMEM((PP, LANES, LANES), F32)],
        compiler_params=_params(("parallel", "arbitrary")),
        name="rwkv_state_scan",
    )(rh, y0, g, h, bonus, gate, ln_g.reshape(1, -1), ln_b.reshape(1, -1))


def _ct_kernel(w_ref, o_ref):
    o_ref[...] = w_ref[0].T.astype(BF16)


def _ct_patch_kernel(w_ref, patch_ref, o_ref, *, first_patch_row):
    w = w_ref[0]
    rows = lax.broadcasted_iota(jnp.int32, w.shape, 0)
    last = pl.program_id(0) == pl.num_programs(0) - 1
    w = jnp.where(last & (rows >= first_patch_row), patch_ref[...], w)
    o_ref[...] = w.T.astype(BF16)


def _cast_transpose(w_t, l, n_cols, tn, row_of_tile, name, patch=None, first_patch_row=0):
    K = w_t.shape[2]
    in_specs = [pl.BlockSpec((pl.Element(1), pl.Element(tn), pl.Element(K)),
                             lambda j: (l, pl.multiple_of(row_of_tile(j), 16), 0))]
    args = [w_t]
    body = _ct_kernel
    if patch is not None:
        in_specs.append(pl.BlockSpec((tn, K), lambda j: (0, 0)))
        args.append(patch)
        body = functools.partial(_ct_patch_kernel, first_patch_row=first_patch_row)
    return pl.pallas_call(
        body,
        grid=(n_cols // tn,),
        in_specs=in_specs,
        out_specs=pl.BlockSpec((K, tn), lambda j: (0, j)),
        out_shape=jax.ShapeDtypeStruct((K, n_cols), BF16),
        compiler_params=_params(("parallel",)),
        name=name,
    )(*args)


def _pack_in_proj(w_t, l, vdown):
    tn = 512
    n_plain = IN_IDX // tn
    w_b = _cast_transpose(
        w_t, l, ZB_WIDTH, tn,
        lambda j: jnp.where(j < n_plain, j * tn, IN_GATE + (j - n_plain) * tn), "pack_w_b")
    n_own = ZR_XV - (ZR_WIDTH - tn)
    patch = jnp.pad(vdown.T, ((n_own, 0), (0, 0)))
    w_r = _cast_transpose(w_t, l, ZR_WIDTH, tn, lambda j: IN_RWKV + j * tn, "pack_w_r",
                          patch=patch, first_patch_row=n_own)
    w_i = _cast_transpose(w_t, l, ZI_WIDTH, 256, lambda j: IN_IDX + j * 256, "pack_w_i")
    return w_b, w_r, w_i


def _rows_at(w, start, height):
    return jnp.pad(w, ((start, height - start - w.shape[0]), (0, 0)))


def kernel(x, w_in, norm_mix, dw_weight, dw_bias, conv_ln_g, conv_ln_b, w_conv_out, w_att_out, rwkv_mu, rwkv_w0, rwkv_w2, rwkv_a0, rwkv_a2, rwkv_g2, rwkv_k_k, rwkv_k_a, rwkv_r_k, rwkv_ln_g, rwkv_ln_b, vres_down, vres_mu, vres_up, vres_bias, w_rwkv_out, w_out, norm_ffn, w_ffn_gate, w_ffn_up, w_ffn_down, norm_final):
    B, L, D = x.shape
    assert B == 1 and D == D_MODEL and L % 256 == 0
    depth = w_in.shape[0]
    xs = x.reshape(L, D)
    assert w_in.shape[1:] == (D_MODEL, D_IN)
    w_t = jnp.swapaxes(w_in, 1, 2)
    wc_b, wa_b, wr_b, wo_b = (w.astype(BF16) for w in (w_conv_out, w_att_out, w_rwkv_out, w_out))
    wg_b, wu_b, wd_b = (w.astype(BF16) for w in (w_ffn_gate, w_ffn_up, w_ffn_down))
    v_first = None
    for l in range(depth):
        vdown = vres_down[l - 1] if l > 0 else jnp.zeros((D, LORA_MV), F32)
        w_b, w_r, w_i = _pack_in_proj(w_t, l, vdown)
        mu_p = jnp.concatenate([rwkv_mu[l], vres_mu[l - 1] if l > 0 else jnp.zeros((LORA_MV,), F32)])
        w2_p = _rows_at(rwkv_w2[l], 0, LANES).astype(BF16)
        a2_p = _rows_at(rwkv_a2[l], ZR_XA - ZR_XA_WIN[0], ZR_XA_WIN[1] - ZR_XA_WIN[0]).astype(BF16)
        g2_p = _rows_at(rwkv_g2[l], ZR_XG - ZR_XG_WIN[0], ZR_XG_WIN[1] - ZR_XG_WIN[0]).astype(BF16)

        h = _rms_norm(xs, norm_mix[l], BF16)
        zb = _matmul(h, w_b, BF16, 1024, 512, "in_proj_bf16")
        zr = _matmul(h, w_r, F32, 1024, 512, "in_proj_rwkv")
        zi = _matmul(h, w_i, F32, 1024, 256, "in_proj_index")

        a_mix = _conformer_conv(zb, dw_weight[l], dw_bias[l], conv_ln_g[l], conv_ln_b[l])

        qr, kr, iqr, ik2, iws = _att_prep(zb, zi)
        b_mix = _dsa_attention(qr, iqr, iws, ik2, kr, zb)

        vres = None
        if l > 0:
            vup_p = _rows_at(vres_up[l - 1], ZR_XV - ZR_XV_WIN[0], LANES).astype(BF16)
            vres = (v_first, vup_p, vres_bias[l - 1])
        rt, at, bt, kt, v_rwkv, bonus, gate, pc = _rwkv_prep(
            zr, mu_p, rwkv_w0[l], w2_p, rwkv_a0[l], a2_p, g2_p,
            rwkv_k_k[l], rwkv_k_a[l], rwkv_r_k[l].reshape(-1), vres)
        if l == 0:
            v_first = v_rwkv
        rh, y0, g_mat, h_mat = _scan_a(rt, at, bt, kt, v_rwkv, pc)
        c_mix = _scan_b(rh, y0, g_mat, h_mat, bonus, gate, rwkv_ln_g[l], rwkv_ln_b[l])

        merged = _merge(a_mix, b_mix, c_mix, wc_b, wa_b, wr_b, l, zb, 1024, 512)
        xs = _matmul_residual(merged, wo_b, l, xs, 1024, 512, "out_proj")

        h2 = _rms_norm(xs, norm_ffn[l], BF16)
        act = _ffn1(h2, wg_b, wu_b, l, 1024, 256)
        xs = _matmul_residual(act, wd_b, l, xs, 512, 256, "ffn_down")
    return _rms_norm(xs, norm_final, F32).reshape(B, L, D)
```
